```python
import jax
import jax.numpy as jnp
from jax import lax
import numpy as np

D_MODEL = 1024
BATCH = 8
SEQ = 4096
DEPTH = 2

CTX_LEN = 256
GRID_W = 64
HEAD_DIM = 64
MIX_WIDTH = D_MODEL
POOL_GROUPS = 4
POOL_WIDTH = MIX_WIDTH // 4
POOL_CH = POOL_WIDTH // POOL_GROUPS
POOL_WINDOWS = (2, 4, 8, 16)
GQA_HEADS = (MIX_WIDTH - POOL_WIDTH) // HEAD_DIM
GQA_KV_HEADS = 4
GQA_GROUP = GQA_HEADS // GQA_KV_HEADS
GQA_Q = GQA_HEADS * HEAD_DIM
GQA_KV = GQA_KV_HEADS * HEAD_DIM
EVEN_IN = POOL_WIDTH + GQA_Q + 2 * GQA_KV
FOURIER_GROUPS = 4
FOURIER_WIDTH = MIX_WIDTH // 4
FOURIER_CH = FOURIER_WIDTH // FOURIER_GROUPS
NA_HEADS = (MIX_WIDTH - FOURIER_WIDTH) // HEAD_DIM
NA_WIDTH = NA_HEADS * HEAD_DIM
ODD_IN = FOURIER_WIDTH + 3 * NA_WIDTH
MAX_WIN_R = 8
WIN_C = 16
Q_BLOCK = 128
ROPE_THETA = 10000.0
ROPE_FREQS = HEAD_DIM // 4
FFN_DIM = 2816
N_EXPERTS = 8
TOP_K = 2
N_EVEN = (DEPTH + 1) // 2
N_ODD = DEPTH // 2
DEEPNORM_ALPHA = (2 * DEPTH) ** 0.25
DEEPNORM_BETA = (8 * DEPTH) ** -0.25
LN_EPS = 1e-6
RMS_EPS = 1e-6

kernel_name = 'hybrid_pool_gqa_fourier_natten_moe_dit'


def _layer_norm(x, g, b):
    xf = x.astype(jnp.float32)
    mu = jnp.mean(xf, axis=-1, keepdims=True)
    var = jnp.mean(jnp.square(xf - mu), axis=-1, keepdims=True)
    return ((xf - mu) * lax.rsqrt(var + LN_EPS) * g.astype(jnp.float32) + b.astype(jnp.float32)).astype(x.dtype)


def _rms_norm(x, g):
    xf = x.astype(jnp.float32)
    inv = lax.rsqrt(jnp.mean(jnp.square(xf), axis=-1, keepdims=True) + RMS_EPS)
    return (xf * inv * g.astype(jnp.float32)).astype(x.dtype)


def _modulate(u, shift, scale):
    return u * (1 + scale) + shift


def _swiglu(u, w1, w3, w2):
    return (jax.nn.silu(u @ w1) * (u @ w3)) @ w2


def _axial_rope_tables(n):
    t = jnp.arange(n, dtype=jnp.int32)
    row = (t // GRID_W).astype(jnp.float32)
    col = (t % GRID_W).astype(jnp.float32)
    inv_freq = jnp.power(ROPE_THETA, -jnp.arange(ROPE_FREQS, dtype=jnp.float32) / ROPE_FREQS)
    ang = jnp.stack([row[:, None] * inv_freq, col[:, None] * inv_freq], axis=1)
    return jnp.cos(ang), jnp.sin(ang)


def _apply_axial_rope(u, cos, sin):
    b, n, h, d = u.shape
    uf = u.astype(jnp.float32).reshape(b, n, h, 2, 2, ROPE_FREQS)
    u1, u2 = uf[..., 0, :], uf[..., 1, :]
    cs, sn = cos[None, :, None], sin[None, :, None]
    out = jnp.stack([u1 * cs - u2 * sn, u2 * cs + u1 * sn], axis=-2)
    return out.reshape(b, n, h, d).astype(u.dtype)


def _softmax_attend(q, k, v):
    s = jnp.einsum('bqkgd,bskd->bkgqs', q, k, preferred_element_type=jnp.float32) * (q.shape[-1] ** -0.5)
    p = jax.nn.softmax(s, axis=-1).astype(v.dtype)
    return jnp.einsum('bkgqs,bskd->bqkgd', p, v)


def _blocked_attention(q, k, v):
    b, n, kvh, g, d = q.shape
    nblk = n // Q_BLOCK
    qb = jnp.moveaxis(q.reshape(b, nblk, Q_BLOCK, kvh, g, d), 1, 0)
    ob = lax.map(lambda qi: _softmax_attend(qi, k, v), qb)
    return jnp.moveaxis(ob, 0, 1).reshape(b, n, kvh, g, d)


def _multiscale_pool(u):
    b, n, g, ch = u.shape
    s = jnp.cumsum(u.astype(jnp.float32), axis=1)
    s = jnp.concatenate([jnp.zeros((b, 1, g, ch), jnp.float32), s], axis=1)
    t = jnp.arange(n, dtype=jnp.int32)[:, None]
    win = jnp.asarray(POOL_WINDOWS, dtype=jnp.int32)[None, :]
    lo = jnp.clip(t - win // 2, 0, n)
    hi = jnp.clip(t - win // 2 + win, 0, n)
    gidx = jnp.arange(g, dtype=jnp.int32)[None, :]
    total = s[:, hi, gidx] - s[:, lo, gidx]
    mean = total / (hi - lo).astype(jnp.float32)[None, :, :, None]
    return (mean - u.astype(jnp.float32)).astype(u.dtype)


def _neighbourhood_attention(q, k, v, k_ctx, v_ctx, rpb):
    nb, n, nh, d = q.shape
    rows = n // GRID_W
    win_r = min(MAX_WIN_R, rows)
    kg = k.reshape(nb, rows, GRID_W, nh, d)
    vg = v.reshape(nb, rows, GRID_W, nh, d)
    qg = jnp.moveaxis(q.reshape(nb, rows, GRID_W, nh, d), 1, 0)
    col = jnp.arange(GRID_W, dtype=jnp.int32)
    col_start = jnp.clip(col - WIN_C // 2, 0, GRID_W - WIN_C)
    col_idx = col_start[:, None] + jnp.arange(WIN_C, dtype=jnp.int32)[None, :]
    col_off = col_idx - col[:, None] + (WIN_C - 1)
    scale = d ** -0.5
    n_loc = win_r * WIN_C

    def one_row(args):
        r, q_row = args
        r_start = jnp.clip(r - win_r // 2, 0, rows - win_r)
        k_rows = lax.dynamic_slice_in_dim(kg, r_start, win_r, axis=1)
        v_rows = lax.dynamic_slice_in_dim(vg, r_start, win_r, axis=1)
        k_win = k_rows[:, :, col_idx]
        v_win = v_rows[:, :, col_idx]
        row_off = r_start + jnp.arange(win_r, dtype=jnp.int32) - r + (MAX_WIN_R - 1)
        bias = jnp.transpose(rpb[:, row_off][:, :, col_off], (0, 2, 1, 3)).astype(jnp.float32)
        s_loc = jnp.einsum('bqhd,brqchd->bhqrc', q_row, k_win, preferred_element_type=jnp.float32) * scale + bias[None]
        s_ctx = jnp.einsum('bqhd,bkhd->bhqk', q_row, k_ctx, preferred_element_type=jnp.float32) * scale
        s = jnp.concatenate([s_loc.reshape(nb, nh, GRID_W, n_loc), s_ctx], axis=-1)
        p = jax.nn.softmax(s, axis=-1).astype(v.dtype)
        p_loc = p[..., :n_loc].reshape(nb, nh, GRID_W, win_r, WIN_C)
        p_ctx = p[..., n_loc:]
        return (jnp.einsum('bhqrc,brqchd->bqhd', p_loc, v_win)
                + jnp.einsum('bhqk,bkhd->bqhd', p_ctx, v_ctx))

    o = lax.map(one_row, (jnp.arange(rows, dtype=jnp.int32), qg))
    return jnp.moveaxis(o, 0, 1).reshape(nb, n, nh * d)


def _moe_swiglu(u, w_router, w1, w3, w2):
    b, n, d = u.shape
    t = u.reshape(b * n, d)
    logits = (t @ w_router).astype(jnp.float32)
    top_val, top_idx = lax.top_k(logits, TOP_K)
    top_w = jax.nn.softmax(top_val, axis=-1)
    gates = jnp.sum(jax.nn.one_hot(top_idx, N_EXPERTS, dtype=jnp.float32) * top_w[..., None], axis=1)
    out = jnp.zeros_like(t)
    for e in range(N_EXPERTS):
        out = out + gates[:, e:e + 1].astype(t.dtype) * _swiglu(t, w1[e], w3[e], w2[e])
    return out.reshape(b, n, d)


def _mixer_pool_gqa(h, hc, w_in, pool_w, pool_scale, q_gain, k_gain, w_out, with_ctx):
    def project(u):
        b, n, _ = u.shape
        p = u @ w_in
        a, q, k, v = jnp.split(p, [POOL_WIDTH, POOL_WIDTH + GQA_Q, POOL_WIDTH + GQA_Q + GQA_KV], axis=-1)
        q = _rms_norm(q.reshape(b, n, GQA_HEADS, HEAD_DIM), q_gain)
        k = _rms_norm(k.reshape(b, n, GQA_KV_HEADS, HEAD_DIM), k_gain)
        return a, q, k, v.reshape(b, n, GQA_KV_HEADS, HEAD_DIM)

    def pool_branch(a):
        b, n, _ = a.shape
        m = _multiscale_pool(a.reshape(b, n, POOL_GROUPS, POOL_CH))
        m = jnp.einsum('bngc,gcd->bngd', m, pool_w)
        return m.reshape(b, n, POOL_WIDTH) * pool_scale

    b, n, _ = h.shape
    a, q, k, v = project(h)
    a_c, q_c, k_c, v_c = project(hc)
    cos, sin = _axial_rope_tables(n)
    q = _apply_axial_rope(q, cos, sin)
    k = _apply_axial_rope(k, cos, sin)
    k_all = jnp.concatenate([k, k_c], axis=1)
    v_all = jnp.concatenate([v, v_c], axis=1)
    o = _blocked_attention(q.reshape(b, n, GQA_KV_HEADS, GQA_GROUP, HEAD_DIM), k_all, v_all)
    y = jnp.concatenate([pool_branch(a), o.reshape(b, n, GQA_Q)], axis=-1) @ w_out
    yc = None
    if with_ctx:
        lc = hc.shape[1]
        oc = _softmax_attend(q_c.reshape(b, lc, GQA_KV_HEADS, GQA_GROUP, HEAD_DIM), k_c, v_c)
        yc = jnp.concatenate([pool_branch(a_c), oc.reshape(b, lc, GQA_Q)], axis=-1) @ w_out
    return y, yc


def _mixer_fourier_na(h, hc, w_in, fourier_gain, rpb, w_out, with_ctx):
    def project(u):
        b, n, _ = u.shape
        p = u @ w_in
        f, q, k, v = jnp.split(p, [FOURIER_WIDTH, FOURIER_WIDTH + NA_WIDTH, FOURIER_WIDTH + 2 * NA_WIDTH], axis=-1)
        heads = lambda z: z.reshape(b, n, NA_HEADS, HEAD_DIM)
        return f, heads(q), heads(k), heads(v)

    def fourier_branch(f):
        b, n, _ = f.shape
        u = _rms_norm(f.reshape(b, n, FOURIER_GROUPS, FOURIER_CH), fourier_gain)
        spec = jnp.fft.fftn(u.astype(jnp.float32), axes=(1, 3), norm='ortho')
        return jnp.real(spec).astype(f.dtype).reshape(b, n, FOURIER_WIDTH)

    f, q, k, v = project(h)
    f_c, q_c, k_c, v_c = project(hc)
    o = _neighbourhood_attention(q, k, v, k_c, v_c, rpb)
    y = jnp.concatenate([fourier_branch(f), o], axis=-1) @ w_out
    yc = None
    if with_ctx:
        b, lc = hc.shape[:2]
        oc = _softmax_attend(q_c[:, :, :, None], k_c, v_c).reshape(b, lc, NA_WIDTH)
        yc = jnp.concatenate([fourier_branch(f_c), oc], axis=-1) @ w_out
    return y, yc


def setup_inputs(seed: int = 0) -> dict:
    key = jax.random.key(seed)
    ks = iter(jax.random.split(key, 32))
    D = D_MODEL

    def nrm(shape, scale):
        return jax.random.normal(next(ks), shape, jnp.float32) * scale

    def gain(shape):
        return 1.0 + nrm(shape, 0.02)

    return {
        'x': nrm((BATCH, SEQ, D), 1.0),
        'c': nrm((BATCH, D), 1.0),
        'ctx': nrm((BATCH, CTX_LEN, D), 1.0),
        'c_ctx': nrm((D,), 1.0),
        'ada_w': nrm((DEPTH, D, 6 * D), 0.5 * D ** -0.5),
        'ada_b': nrm((DEPTH, 6 * D), 0.01),
        'ln_mix_g': gain((DEPTH, D)),
        'ln_mix_b': nrm((DEPTH, D), 0.01),
        'ln_ffn_g': gain((DEPTH, D)),
        'ln_ffn_b': nrm((DEPTH, D), 0.01),
        'w_out': nrm((DEPTH, MIX_WIDTH, D), MIX_WIDTH ** -0.5 * DEEPNORM_BETA),
        'ev_w_in': nrm((N_EVEN, D, EVEN_IN), D ** -0.5),
        'ev_pool_w': nrm((N_EVEN, POOL_GROUPS, POOL_CH, POOL_CH), POOL_CH ** -0.5),
        'ev_pool_scale': gain((N_EVEN, POOL_WIDTH)),
        'ev_q_gain': gain((N_EVEN, HEAD_DIM)),
        'ev_k_gain': gain((N_EVEN, HEAD_DIM)),
        'ev_ffn_w1': nrm((N_EVEN, D, FFN_DIM), D ** -0.5),
        'ev_ffn_w3': nrm((N_EVEN, D, FFN_DIM), D ** -0.5),
        'ev_ffn_w2': nrm((N_EVEN, FFN_DIM, D), FFN_DIM ** -0.5 * DEEPNORM_BETA),
        'od_w_in': nrm((N_ODD, D, ODD_IN), D ** -0.5),
        'od_fourier_gain': gain((N_ODD, FOURIER_GROUPS, FOURIER_CH)),
        'od_rpb': nrm((N_ODD, NA_HEADS, 2 * MAX_WIN_R - 1, 2 * WIN_C - 1), 0.1),
        'od_router': nrm((N_ODD, D, N_EXPERTS), D ** -0.5),
        'od_exp_w1': nrm((N_ODD, N_EXPERTS, D, FFN_DIM), D ** -0.5),
        'od_exp_w3': nrm((N_ODD, N_EXPERTS, D, FFN_DIM), D ** -0.5),
        'od_exp_w2': nrm((N_ODD, N_EXPERTS, FFN_DIM, D), FFN_DIM ** -0.5 * DEEPNORM_BETA),
    }


def reference(x, c, ctx, c_ctx, ada_w, ada_b, ln_mix_g, ln_mix_b, ln_ffn_g, ln_ffn_b, w_out,
              ev_w_in, ev_pool_w, ev_pool_scale, ev_q_gain, ev_k_gain, ev_ffn_w1, ev_ffn_w3, ev_ffn_w2,
              od_w_in, od_fourier_gain, od_rpb, od_router, od_exp_w1, od_exp_w3, od_exp_w2):
    alpha = DEEPNORM_ALPHA
    for layer in range(DEPTH):
        with_ctx = layer < DEPTH - 1
        i = layer // 2
        mod = jax.nn.silu(c) @ ada_w[layer] + ada_b[layer]
        mod_c = jax.nn.silu(c_ctx) @ ada_w[layer] + ada_b[layer]
        sh1, sc1, g1, sh2, sc2, g2 = jnp.split(mod[:, None, :], 6, axis=-1)
        csh1, csc1, cg1, csh2, csc2, cg2 = jnp.split(mod_c, 6, axis=-1)
        h = _modulate(x, sh1, sc1)
        hc = _modulate(ctx, csh1, csc1)
        if layer % 2 == 0:
            y, yc = _mixer_pool_gqa(h, hc, ev_w_in[i], ev_pool_w[i], ev_pool_scale[i],
                                    ev_q_gain[i], ev_k_gain[i], w_out[layer], with_ctx)
            ffn = lambda u: _swiglu(u, ev_ffn_w1[i], ev_ffn_w3[i], ev_ffn_w2[i])
        else:
            y, yc = _mixer_fourier_na(h, hc, od_w_in[i], od_fourier_gain[i], od_rpb[i],
                                      w_out[layer], with_ctx)
            ffn = lambda u: _moe_swiglu(u, od_router[i], od_exp_w1[i], od_exp_w3[i], od_exp_w2[i])
        x = _layer_norm(alpha * x + g1 * y, ln_mix_g[layer], ln_mix_b[layer])
        x = _layer_norm(alpha * x + g2 * ffn(_modulate(x, sh2, sc2)), ln_ffn_g[layer], ln_ffn_b[layer])
        if with_ctx:
            ctx = _layer_norm(alpha * ctx + cg1 * yc, ln_mix_g[layer], ln_mix_b[layer])
            ctx = _layer_norm(alpha * ctx + cg2 * ffn(_modulate(ctx, csh2, csc2)), ln_ffn_g[layer], ln_ffn_b[layer])
    return x
```

```python
import functools
import math

import numpy as np
import jax
import jax.numpy as jnp
from jax import lax
from jax.experimental import pallas as pl
from jax.experimental.pallas import tpu as pltpu

F32 = jnp.float32
BF16 = jnp.bfloat16

D_MODEL = 1024
HEAD_DIM = 64
GRID_W = 64
POOL_WIDTH = 256
POOL_WINDOWS = (2, 4, 8, 16)
GQA_HEADS = 12
GQA_KV_HEADS = 4
GQA_GROUP = 3
GQA_Q = 768
GQA_KV = 256
FOURIER_WIDTH = 256
NA_HEADS = 12
NA_WIDTH = 768
MAX_WIN_R = 8
WIN_C = 16
ROPE_THETA = 10000.0
ROPE_FREQS = 16
FFN_DIM = 2816
N_EXPERTS = 8
LN_EPS = 1e-6
RMS_EPS = 1e-6

LANES = 128
MOD_ROWS = 16
VMEM_LIMIT = 56 * 1024 * 1024
NA_QROWS = 8
NA_KROWS = 16
MOE_TILE = 256


def _cparams(sem):
    return pltpu.CompilerParams(dimension_semantics=sem, vmem_limit_bytes=VMEM_LIMIT)


def _const_spec(shape):
    nd = len(shape)
    return pl.BlockSpec(shape, lambda *_: (0,) * nd, pipeline_mode=pl.Buffered(1))


def _layer_norm(v, g, b):
    mu = jnp.mean(v, axis=-1, keepdims=True)
    d = v - mu
    var = jnp.mean(d * d, axis=-1, keepdims=True)
    return d * lax.rsqrt(var + LN_EPS) * g + b


def _dot(a, b):
    return jnp.dot(a, b, preferred_element_type=F32)


def _dot_nt(a, b):
    return lax.dot_general(a, b, (((1,), (1,)), ((), ())), preferred_element_type=F32)


def _head_slot(shape):
    return lax.broadcasted_iota(jnp.int32, shape, len(shape) - 1) // HEAD_DIM


def _mod_kernel(c_ref, w_ref, b_ref, o_ref):
    c = c_ref[...]
    s = c * jax.nn.sigmoid(c)
    o_ref[...] = jnp.dot(s, w_ref[...], preferred_element_type=F32,
                         precision=lax.Precision.HIGHEST) + b_ref[...]


def _modulation(c_all, ada_w, ada_b):
    depth = ada_w.shape[0]
    d = D_MODEL
    out = pl.pallas_call(
        _mod_kernel,
        grid=(depth, 6),
        in_specs=[
            pl.BlockSpec((MOD_ROWS, d), lambda l, n: (0, 0)),
            pl.BlockSpec((None, d, d), lambda l, n: (l, 0, n)),
            pl.BlockSpec((None, 1, d), lambda l, n: (l, 0, n)),
        ],
        out_specs=pl.BlockSpec((None, MOD_ROWS, d), lambda l, n: (l, 0, n)),
        out_shape=jax.ShapeDtypeStruct((depth, MOD_ROWS, 6 * d), F32),
        compiler_params=_cparams(("arbitrary", "arbitrary")),
        name="adaln_mod",
    )(c_all, ada_w, ada_b.reshape(depth, 1, 6 * d))
    return out.reshape(depth, MOD_ROWS, 6, d)


def _mod_spec(tiles_per_group, fixed_row):
    if fixed_row is None:
        return pl.BlockSpec((1, 6, D_MODEL), lambda t: (t // tiles_per_group, 0, 0))
    return pl.BlockSpec((1, 6, D_MODEL), lambda t: (fixed_row, 0, 0))


def _group_norm(u, gmat, gain):
    ms = _dot((u * u).astype(BF16), gmat)
    return u * lax.rsqrt(ms + RMS_EPS) * gain


def _rope(u, cos, sin):
    lane = lax.broadcasted_iota(jnp.int32, u.shape, 1)
    first = (lane // ROPE_FREQS) % 2 == 0
    n = u.shape[1]
    partner = jnp.where(first, pltpu.roll(u, n - ROPE_FREQS, 1), pltpu.roll(u, ROPE_FREQS, 1))
    return u * cos + partner * sin


def _inproj_even_kernel(x_ref, mod_ref, w_ref, g_ref, qg_ref, kg_ref, cos_ref, sin_ref,
                        a_ref, q_ref, k_ref, v_ref, *, rope):
    x = x_ref[...]
    h = (x * (1.0 + mod_ref[0, 1:2, :]) + mod_ref[0, 0:1, :]).astype(BF16)
    p = _dot(h, w_ref[...])
    a_ref[...] = p[:, :POOL_WIDTH]
    gmat = g_ref[...]
    if rope:
        cos = cos_ref[...]
        sin = sin_ref[...]
    for c in range(GQA_Q // 256):
        u = _group_norm(p[:, POOL_WIDTH + 256 * c:POOL_WIDTH + 256 * (c + 1)], gmat, qg_ref[...])
        if rope:
            u = _rope(u, cos, sin)
        q_ref[:, 256 * c:256 * (c + 1)] = (u * (HEAD_DIM ** -0.5)).astype(BF16)
    u = _group_norm(p[:, POOL_WIDTH + GQA_Q:POOL_WIDTH + GQA_Q + GQA_KV], gmat, kg_ref[...])
    if rope:
        u = _rope(u, cos, sin)
    k_ref[...] = u.astype(BF16)
    v_ref[...] = p[:, POOL_WIDTH + GQA_Q + GQA_KV:].astype(BF16)


def _inproj_even(xf, mod, w_in, gmat, qg, kg, cos, sin, *, tm, tiles_per_group, fixed_row, rope):
    rows = xf.shape[0]
    d = D_MODEL
    n_in = w_in.shape[1]
    tile = lambda w: pl.BlockSpec((tm, w), lambda t: (t, 0))
    if rope:
        tab = pl.BlockSpec((tm, 256), lambda t: (t % tiles_per_group, 0))
    else:
        tab = pl.BlockSpec((tm, 256), lambda t: (0, 0))
    return pl.pallas_call(
        functools.partial(_inproj_even_kernel, rope=rope),
        grid=(rows // tm,),
        in_specs=[tile(d), _mod_spec(tiles_per_group, fixed_row), _const_spec((d, n_in)),
                  _const_spec((256, 256)), _const_spec((1, 256)), _const_spec((1, 256)), tab, tab],
        out_specs=[tile(POOL_WIDTH), tile(GQA_Q), tile(GQA_KV), tile(GQA_KV)],
        out_shape=[jax.ShapeDtypeStruct((rows, POOL_WIDTH), F32),
                   jax.ShapeDtypeStruct((rows, GQA_Q), BF16),
                   jax.ShapeDtypeStruct((rows, GQA_KV), BF16),
                   jax.ShapeDtypeStruct((rows, GQA_KV), BF16)],
        compiler_params=_cparams(("parallel",)),
        name="inproj_even",
    )(xf, mod, w_in, gmat, qg, kg, cos, sin)


GQA_KCHUNK = 1024


def _gqa_kernel(*refs, n_lat):
    if n_lat:
        q_ref, k_ref, v_ref, kc_ref, vc_ref, o_ref = refs
    else:
        q_ref, kc_ref, vc_ref, o_ref = refs
    q = q_ref[...]
    slot = _head_slot(q.shape)
    kc = kc_ref[...]
    vc = vc_ref[...]

    def head(g, acc):
        qm = jnp.where(slot == g, q, jnp.zeros_like(q))
        s = _dot_nt(qm, kc)
        m = s.max(axis=-1, keepdims=True)
        p = jnp.exp(s - m)
        l = p.sum(axis=-1, keepdims=True)
        o = _dot(p.astype(BF16), vc)

        def chunk(ci, carry):
            m, l, o = carry
            start = pl.multiple_of(ci * GQA_KCHUNK, GQA_KCHUNK)
            s = _dot_nt(qm, k_ref[pl.ds(start, GQA_KCHUNK), :])
            m_new = jnp.maximum(m, s.max(axis=-1, keepdims=True))
            corr = jnp.exp(m - m_new)
            p = jnp.exp(s - m_new)
            l = l * corr + p.sum(axis=-1, keepdims=True)
            o = o * corr + _dot(p.astype(BF16), v_ref[pl.ds(start, GQA_KCHUNK), :])
            return m_new, l, o

        if n_lat:
            m, l, o = lax.fori_loop(0, n_lat // GQA_KCHUNK, chunk, (m, l, o))
        return jnp.where(slot == g, o / l, acc)

    acc = lax.fori_loop(0, GQA_KV_HEADS, head, jnp.zeros(q.shape, F32))
    o_ref[...] = acc.astype(BF16)


def _gqa_attention(q, k, v, kc, vc, *, tq, n, lc):
    rows = q.shape[0]
    q_per_batch = n if k is not None else lc
    nq = q_per_batch // tq
    qspec = pl.BlockSpec((tq, 256), lambda b, i, c: (b * nq + i, c))
    cspec = pl.BlockSpec((lc, GQA_KV), lambda b, i, c: (b, 0))
    in_specs, args = [qspec], [q]
    if k is not None:
        lspec = pl.BlockSpec((n, GQA_KV), lambda b, i, c: (b, 0))
        in_specs += [lspec, lspec]
        args += [k, v]
    in_specs += [cspec, cspec]
    args += [kc, vc]
    return pl.pallas_call(
        functools.partial(_gqa_kernel, n_lat=n if k is not None else 0),
        grid=(rows // q_per_batch, nq, GQA_Q // 256),
        in_specs=in_specs,
        out_specs=qspec,
        out_shape=jax.ShapeDtypeStruct((rows, GQA_Q), BF16),
        compiler_params=_cparams(("parallel", "arbitrary", "arbitrary")),
        name="gqa_attention",
    )(*args)


POOL_CHUNK = 512
POOL_HALO = 8


def _pool_kernel(a_ref, pw_ref, ps_ref, o_ref, ext_ref, *, n):
    chunk = min(POOL_CHUNK, n)
    zeros = jnp.zeros((POOL_HALO, POOL_WIDTH), F32)
    ext_ref[0:POOL_HALO, :] = zeros
    ext_ref[n + POOL_HALO:n + 2 * POOL_HALO, :] = zeros
    ext_ref[POOL_HALO:n + POOL_HALO, :] = a_ref[...]
    ln = chunk + 2 * POOL_HALO
    lane_grp = _head_slot((chunk, POOL_WIDTH))
    win = jnp.where(lane_grp == 0, POOL_WINDOWS[0],
                    jnp.where(lane_grp == 1, POOL_WINDOWS[1],
                              jnp.where(lane_grp == 2, POOL_WINDOWS[2], POOL_WINDOWS[3])))
    pw = pw_ref[...]
    ps = ps_ref[...]

    def body(ci, carry):
        start = pl.multiple_of(ci * chunk, chunk)
        e = ext_ref[pl.ds(start, ln), :]
        s2 = e + pltpu.roll(e, 1, 0)
        s4 = pltpu.roll(s2, 1, 0) + pltpu.roll(s2, ln - 1, 0)
        s8 = pltpu.roll(s4, 2, 0) + pltpu.roll(s4, ln - 2, 0)
        s16 = pltpu.roll(s8, 4, 0) + pltpu.roll(s8, ln - 4, 0)
        cut = lambda z: z[POOL_HALO:POOL_HALO + chunk, :]
        tot = jnp.where(lane_grp == 0, cut(s2),
                        jnp.where(lane_grp == 1, cut(s4),
                                  jnp.where(lane_grp == 2, cut(s8), cut(s16))))
        t = lax.broadcasted_iota(jnp.int32, (chunk, POOL_WIDTH), 0) + start
        lo = jnp.clip(t - win // 2, 0, n)
        hi = jnp.clip(t - win // 2 + win, 0, n)
        mean = tot / (hi - lo).astype(F32)
        m = mean - cut(e)
        o_ref[pl.ds(start, chunk), :] = (_dot(m.astype(BF16), pw) * ps).astype(BF16)
        return carry

    lax.fori_loop(0, n // chunk, body, 0)


def _pool(a, pw_bd, ps, *, n):
    rows = a.shape[0]
    return pl.pallas_call(
        functools.partial(_pool_kernel, n=n),
        grid=(rows // n,),
        in_specs=[pl.BlockSpec((n, POOL_WIDTH), lambda b: (b, 0)),
                  _const_spec((POOL_WIDTH, POOL_WIDTH)), _const_spec((1, POOL_WIDTH))],
        out_specs=pl.BlockSpec((n, POOL_WIDTH), lambda b: (b, 0)),
        out_shape=jax.ShapeDtypeStruct((rows, POOL_WIDTH), BF16),
        scratch_shapes=[pltpu.VMEM((n + 2 * POOL_HALO, POOL_WIDTH), F32)],
        compiler_params=_cparams(("parallel",)),
        name="multiscale_pool",
    )(a, pw_bd, ps)


def _outproj(xf, ya, yb, mod, wa, wb, ln_g, ln_b, w_router, *, alpha, tm, tiles_per_group, fixed_row):
    rows = xf.shape[0]
    d = D_MODEL
    router = w_router is not None
    tile = lambda w: pl.BlockSpec((tm, w), lambda t: (t, 0))
    in_specs = [tile(d), tile(ya.shape[1]), tile(yb.shape[1]), _mod_spec(tiles_per_group, fixed_row),
                _const_spec(wa.shape), _const_spec(wb.shape), _const_spec((1, d)), _const_spec((1, d))]
    args = [xf, ya, yb, mod, wa, wb, ln_g, ln_b]
    out_specs = [tile(d), tile(d)]
    out_shape = [jax.ShapeDtypeStruct((rows, d), F32),
                 jax.ShapeDtypeStruct((rows, d), F32 if router else BF16)]
    if router:
        in_specs.append(_const_spec(w_router.shape))
        args.append(w_router)
        out_specs.append(tile(LANES))
        out_shape.append(jax.ShapeDtypeStruct((rows, LANES), F32))
    return pl.pallas_call(
        functools.partial(_outproj_body, router=router, alpha=alpha),
        grid=(rows // tm,),
        in_specs=in_specs,
        out_specs=out_specs,
        out_shape=out_shape,
        compiler_params=_cparams(("parallel",)),
        name="outproj_norm",
    )(*args)


def _outproj_body(*refs, router, alpha):
    x_ref, ya_ref, yb_ref, mod_ref, wa_ref, wb_ref, g_ref, b_ref = refs[:8]
    if router:
        wr_ref, x1_ref, h2_ref, route_ref = refs[8:]
    else:
        x1_ref, h2_ref = refs[8:]
    y = _dot(ya_ref[...], wa_ref[...]) + _dot(yb_ref[...], wb_ref[...])
    x1 = _layer_norm(alpha * x_ref[...] + mod_ref[0, 2:3, :] * y, g_ref[...], b_ref[...])
    x1_ref[...] = x1
    h2 = x1 * (1.0 + mod_ref[0, 4:5, :]) + mod_ref[0, 3:4, :]
    h2_ref[...] = h2.astype(h2_ref.dtype)
    if router:
        logits = jnp.dot(h2, wr_ref[...], preferred_element_type=F32, precision=lax.Precision.HIGHEST)
        lane = lax.broadcasted_iota(jnp.int32, logits.shape, 1)
        neg = jnp.float32(-jnp.inf)
        lg = jnp.where(lane < N_EXPERTS, logits, neg)
        m1 = lg.max(axis=-1, keepdims=True)
        i1 = jnp.where(lg == m1, lane, LANES).min(axis=-1, keepdims=True)
        lg2 = jnp.where(lane == i1, neg, lg)
        m2 = lg2.max(axis=-1, keepdims=True)
        i2 = jnp.where(lg2 == m2, lane, LANES).min(axis=-1, keepdims=True)
        e = jnp.exp(m2 - m1)
        w1 = 1.0 / (1.0 + e)
        w2 = e / (1.0 + e)
        route_ref[...] = jnp.where(lane == 0, i1.astype(F32),
                                   jnp.where(lane == 1, i2.astype(F32),
                                             jnp.where(lane == 2, w1,
                                                       jnp.where(lane == 3, w2, 0.0))))


def _ffn_kernel(h_ref, x1_ref, mod_ref, w1_ref, w3_ref, w2_ref, g_ref, b_ref, o_ref, *, alpha):
    h = h_ref[...]
    g = _dot(h, w1_ref[...])
    u = _dot(h, w3_ref[...])
    act = (g * jax.nn.sigmoid(g) * u).astype(BF16)
    f = _dot(act, w2_ref[...])
    o_ref[...] = _layer_norm(alpha * x1_ref[...] + mod_ref[0, 5:6, :] * f, g_ref[...], b_ref[...])


def _ffn(h2, x1, mod, w1, w3, w2, ln_g, ln_b, *, alpha, tm, tiles_per_group, fixed_row):
    rows = x1.shape[0]
    d = D_MODEL
    tile = lambda w: pl.BlockSpec((tm, w), lambda t: (t, 0))
    return pl.pallas_call(
        functools.partial(_ffn_kernel, alpha=alpha),
        grid=(rows // tm,),
        in_specs=[tile(d), tile(d), _mod_spec(tiles_per_group, fixed_row),
                  _const_spec(w1.shape), _const_spec(w3.shape), _const_spec(w2.shape),
                  _const_spec((1, d)), _const_spec((1, d))],
        out_specs=tile(d),
        out_shape=jax.ShapeDtypeStruct((rows, d), F32),
        compiler_params=_cparams(("parallel",)),
        name="swiglu_norm",
    )(h2, x1, mod, w1, w3, w2, ln_g, ln_b)


def _inproj_odd_kernel(x_ref, mod_ref, w_ref, g_ref, fg_ref, cc_ref, sc_ref,
                       fa_ref, fb_ref, q_ref, k_ref, v_ref):
    x = x_ref[...]
    h = (x * (1.0 + mod_ref[0, 1:2, :]) + mod_ref[0, 0:1, :]).astype(BF16)
    p = _dot(h, w_ref[...])
    f = _group_norm(p[:, :FOURIER_WIDTH], g_ref[...], fg_ref[...]).astype(BF16)
    fa_ref[...] = _dot(f, cc_ref[...]).astype(BF16)
    fb_ref[...] = _dot(f, sc_ref[...]).astype(BF16)
    o = FOURIER_WIDTH
    q_ref[...] = (p[:, o:o + NA_WIDTH] * (HEAD_DIM ** -0.5)).astype(BF16)
    k_ref[...] = p[:, o + NA_WIDTH:o + 2 * NA_WIDTH].astype(BF16)
    v_ref[...] = p[:, o + 2 * NA_WIDTH:].astype(BF16)


def _inproj_odd(xf, mod, w_in, gmat, fg, cc, sc, *, tm, tiles_per_group, fixed_row):
    rows = xf.shape[0]
    d = D_MODEL
    tile = lambda w: pl.BlockSpec((tm, w), lambda t: (t, 0))
    widths = (FOURIER_WIDTH, FOURIER_WIDTH, NA_WIDTH, NA_WIDTH, NA_WIDTH)
    return pl.pallas_call(
        _inproj_odd_kernel,
        grid=(rows // tm,),
        in_specs=[tile(d), _mod_spec(tiles_per_group, fixed_row), _const_spec(w_in.shape),
                  _const_spec((256, 256)), _const_spec((1, 256)),
                  _const_spec((256, 256)), _const_spec((256, 256))],
        out_specs=[tile(w) for w in widths],
        out_shape=[jax.ShapeDtypeStruct((rows, w), BF16) for w in widths],
        compiler_params=_cparams(("parallel",)),
        name="inproj_odd",
    )(xf, mod, w_in, gmat, fg, cc, sc)


def _dft_kernel(cn_ref, sn_ref, a_ref, b_ref, o_ref, *, scale):
    o = _dot(cn_ref[...], a_ref[...]) + _dot(sn_ref[...], b_ref[...])
    o_ref[...] = (o * scale).astype(BF16)


def _dft_positions(cn, nsn, fa, fb, *, n, tm):
    rows = fa.shape[0]
    nm = n // tm
    return pl.pallas_call(
        functools.partial(_dft_kernel, scale=float((n * HEAD_DIM) ** -0.5)),
        grid=(nm, rows // n),
        in_specs=[pl.BlockSpec((tm, n), lambda m, b: (m, 0)),
                  pl.BlockSpec((tm, n), lambda m, b: (m, 0)),
                  pl.BlockSpec((n, FOURIER_WIDTH), lambda m, b: (b, 0)),
                  pl.BlockSpec((n, FOURIER_WIDTH), lambda m, b: (b, 0))],
        out_specs=pl.BlockSpec((tm, FOURIER_WIDTH), lambda m, b: (b * nm + m, 0)),
        out_shape=jax.ShapeDtypeStruct((rows, FOURIER_WIDTH), BF16),
        compiler_params=_cparams(("parallel", "arbitrary")),
        name="dft_positions",
    )(cn, nsn, fa, fb)


NA_KPIECE = 256
NA_NPIECE = NA_KROWS * GRID_W // NA_KPIECE


def _na_kernel(tab_ref, *refs):
    del tab_ref
    q_ref = refs[0]
    k_refs = refs[1:1 + NA_NPIECE]
    v_refs = refs[1 + NA_NPIECE:1 + 2 * NA_NPIECE]
    kc_ref, vc_ref, bias_ref, o_ref = refs[1 + 2 * NA_NPIECE:]
    q = q_ref[...]
    slot = _head_slot(q.shape)
    ks = [r[...] for r in k_refs]
    vs = [r[...] for r in v_refs]
    kc = kc_ref[...]
    vc = vc_ref[...]

    def head(hh, acc):
        qm = jnp.where(slot == hh, q, jnp.zeros_like(q))
        s = [_dot_nt(qm, ks[i]) + bias_ref[hh, :, NA_KPIECE * i:NA_KPIECE * (i + 1)]
             for i in range(NA_NPIECE)]
        s.append(_dot_nt(qm, kc))
        m = s[0].max(axis=-1, keepdims=True)
        for si in s[1:]:
            m = jnp.maximum(m, si.max(axis=-1, keepdims=True))
        l = jnp.zeros_like(m)
        o = jnp.zeros(q.shape, F32)
        for si, v in zip(s, vs + [vc]):
            p = jnp.exp(si - m)
            l = l + p.sum(axis=-1, keepdims=True)
            o = o + _dot(p.astype(BF16), v)
        return jnp.where(slot == hh, o / l, acc)

    acc = lax.fori_loop(0, 256 // HEAD_DIM, head, jnp.zeros(q.shape, F32))
    o_ref[...] = acc.astype(BF16)


def _na_tables(rows):
    nj = rows // NA_QROWS
    variants, var_of_j, kb_of_j = [], [], []
    for j in range(nj):
        ks = min(max(NA_QROWS * j - MAX_WIN_R // 2, 0), rows - NA_KROWS)
        d = ks - NA_QROWS * j + MAX_WIN_R - 1
        ls = tuple(min(max(NA_QROWS * j + i - MAX_WIN_R // 2, 0), rows - MAX_WIN_R) - ks
                   for i in range(NA_QROWS))
        key = (d, ls)
        if key not in variants:
            variants.append(key)
        var_of_j.append(variants.index(key))
        kb_of_j.append(ks * GRID_W // NA_KPIECE)
    return variants, np.asarray([var_of_j, kb_of_j], np.int32)


def _na_bias(rpb, variants):
    nh = rpb.shape[0]
    qc = np.arange(GRID_W)[:, None]
    kc = np.arange(GRID_W)[None, :]
    cs = np.clip(qc - WIN_C // 2, 0, GRID_W - WIN_C)
    col_ok = (kc >= cs) & (kc < cs + WIN_C)
    col_idx = np.clip(kc - qc + WIN_C - 1, 0, 2 * WIN_C - 2)
    col_sel = (col_idx[..., None] == np.arange(2 * WIN_C - 1)) & col_ok[..., None]
    tcol = jnp.einsum("hab,qkb->haqk", rpb.astype(F32), jnp.asarray(col_sel, F32),
                      precision=lax.Precision.HIGHEST)
    i = np.arange(NA_QROWS)[:, None]
    r = np.arange(NA_KROWS)[None, :]
    row_sel, ok = [], []
    for d, ls in variants:
        lsa = np.asarray(ls)[:, None]
        row_ok = (r >= lsa) & (r < lsa + MAX_WIN_R)
        a = np.clip(r - i + d, 0, 2 * MAX_WIN_R - 2)
        row_sel.append((a[..., None] == np.arange(2 * MAX_WIN_R - 1)) & row_ok[..., None])
        ok.append(row_ok[:, None, :, None] & col_ok[None, :, None, :])
    row_sel = jnp.asarray(np.stack(row_sel), F32)
    ok = jnp.asarray(np.stack(ok))
    bias = jnp.einsum("vira,haqk->vhiqrk", row_sel, tcol, precision=lax.Precision.HIGHEST)
    bias = jnp.where(ok[:, None], bias, -jnp.inf)
    nq = NA_QROWS * GRID_W
    return bias.reshape(len(variants), nh, nq, NA_KROWS * GRID_W)


def _na_attention(q, k, v, kc, vc, bias, tab, *, n, lc):
    rows = q.shape[0]
    nb = rows // n
    nq = NA_QROWS * GRID_W
    nj = n // nq
    kblocks = n // NA_KPIECE
    nchunk = NA_WIDTH // 256
    qspec = pl.BlockSpec((nq, 256), lambda j, c, b, tab: (b * nj + j, c))

    def piece(i):
        return pl.BlockSpec((NA_KPIECE, 256), lambda j, c, b, tab: (b * kblocks + tab[1, j] + i, c))

    cspec = pl.BlockSpec((lc, 256), lambda j, c, b, tab: (b, c))
    bspec = pl.BlockSpec((None, 256 // HEAD_DIM, nq, NA_KROWS * GRID_W),
                         lambda j, c, b, tab: (tab[0, j], c, 0, 0))
    grid_spec = pltpu.PrefetchScalarGridSpec(
        num_scalar_prefetch=1,
        grid=(nj, nchunk, nb),
        in_specs=[qspec] + [piece(i) for i in range(NA_NPIECE)] * 2 + [cspec, cspec, bspec],
        out_specs=qspec,
    )
    return pl.pallas_call(
        _na_kernel,
        grid_spec=grid_spec,
        out_shape=jax.ShapeDtypeStruct((rows, NA_WIDTH), BF16),
        compiler_params=_cparams(("arbitrary", "arbitrary", "arbitrary")),
        name="neighbourhood_attention",
    )(tab, q, *([k] * NA_NPIECE), *([v] * NA_NPIECE), kc, vc, bias)


def _moe_kernel(te_ref, tu_ref, src_ref, dst_ref, h_hbm, w1_ref, w3_ref, w2_ref, y_hbm,
                xbuf, ybuf, gsem, ssem):
    del te_ref
    i = pl.program_id(0)
    tr = xbuf.shape[0]

    def gather_copy(r):
        return pltpu.make_async_copy(h_hbm.at[pl.ds(src_ref[0, r], 1), :], xbuf.at[pl.ds(r, 1), :], gsem)

    def scatter_copy(r):
        return pltpu.make_async_copy(ybuf.at[pl.ds(r, 1), :], y_hbm.at[pl.ds(dst_ref[0, r], 1), :], ssem)

    def each_row(fn):
        def body(r, carry):
            fn(r)
            return carry
        lax.fori_loop(0, tr, body, 0)

    @pl.when(i == 0)
    def _():
        ybuf[...] = jnp.zeros(ybuf.shape, F32)
        slot_rows = y_hbm.shape[0] // 2
        for s in range(2):
            spare = pltpu.make_async_copy(ybuf, y_hbm.at[pl.ds((s + 1) * slot_rows - tr, tr), :], ssem)
            spare.start()
            spare.wait()

    @pl.when(tu_ref[i] == 1)
    def _():
        each_row(lambda r: gather_copy(r).start())
        each_row(lambda r: gather_copy(r).wait())
        x = xbuf[...].astype(BF16)
        g = _dot(x, w1_ref[...])
        u = _dot(x, w3_ref[...])
        act = (g * jax.nn.sigmoid(g) * u).astype(BF16)
        ybuf[...] = _dot(act, w2_ref[...])
        each_row(lambda r: scatter_copy(r).start())
        each_row(lambda r: scatter_copy(r).wait())


def _moe_experts(h2, te, tu, src, dst, w1, w3, w2, *, out_rows):
    d = D_MODEL
    n_tiles, _, tr = src.shape
    wspec = lambda shp: pl.BlockSpec((None,) + shp, lambda i, te, tu: (te[i], 0, 0))
    ispec = pl.BlockSpec((None, 1, tr), lambda i, te, tu: (i, 0, 0), memory_space=pltpu.SMEM)
    grid_spec = pltpu.PrefetchScalarGridSpec(
        num_scalar_prefetch=2,
        grid=(n_tiles,),
        in_specs=[ispec, ispec, pl.BlockSpec(memory_space=pl.ANY),
                  wspec((d, FFN_DIM)), wspec((d, FFN_DIM)), wspec((FFN_DIM, d))],
        out_specs=pl.BlockSpec(memory_space=pl.ANY),
        scratch_shapes=[pltpu.VMEM((tr, d), F32), pltpu.VMEM((tr, d), F32),
                        pltpu.SemaphoreType.DMA(()), pltpu.SemaphoreType.DMA(())],
    )
    return pl.pallas_call(
        _moe_kernel,
        grid_spec=grid_spec,
        out_shape=jax.ShapeDtypeStruct((out_rows, d), F32),
        compiler_params=_cparams(("arbitrary",)),
        name="moe_experts",
    )(te, tu, src, dst, h2, w1, w3, w2)


def _route_tables(idx, n_tok, tr):
    n_pair = 2 * n_tok
    n_tiles = n_pair // tr + N_EXPERTS
    e_flat = idx.reshape(-1)
    order = jnp.argsort(e_flat, stable=True).astype(jnp.int32)
    counts = jnp.sum((e_flat[:, None] == jnp.arange(N_EXPERTS)[None, :]).astype(jnp.int32), axis=0)
    tiles_e = (counts + tr - 1) // tr
    tile_end = jnp.cumsum(tiles_e)
    tile_start = tile_end - tiles_e
    ustart = jnp.cumsum(counts) - counts
    total = tile_end[-1]
    tile = jnp.arange(n_tiles, dtype=jnp.int32)
    tu = (tile < total).astype(jnp.int32)
    te = jnp.sum((jnp.minimum(tile, total - 1)[:, None] >= tile_end[None, :]).astype(jnp.int32), axis=1)
    te = jnp.clip(te, 0, N_EXPERTS - 1)
    r = jnp.arange(n_tiles * tr, dtype=jnp.int32)
    e_r = te[r // tr]
    qpos = r - tile_start[e_r] * tr
    valid = (qpos < counts[e_r]) & (tu[r // tr] == 1)
    pair = order[jnp.clip(ustart[e_r] + qpos, 0, n_pair - 1)]
    tok = pair // 2
    slot = pair % 2
    src = jnp.where(valid, tok, 0).astype(jnp.int32)
    dst = jnp.where(valid, slot * (n_tok + tr) + tok, n_tok + r % tr).astype(jnp.int32)
    return te, tu, src.reshape(n_tiles, 1, tr), dst.reshape(n_tiles, 1, tr)


def _final_kernel(x1_ref, ya_ref, yb_ref, route_ref, mod_ref, g_ref, b_ref, o_ref, *, alpha):
    r = route_ref[...]
    f = r[:, 2:3] * ya_ref[0] + r[:, 3:4] * yb_ref[0]
    o_ref[...] = _layer_norm(alpha * x1_ref[...] + mod_ref[0, 5:6, :] * f, g_ref[...], b_ref[...])


def _moe_combine(x1, y2, route, mod, ln_g, ln_b, *, alpha, tm, tiles_per_group):
    rows = x1.shape[0]
    d = D_MODEL
    tile = lambda w: pl.BlockSpec((tm, w), lambda t: (t, 0))
    yspec = lambda s: pl.BlockSpec((1, tm, d), lambda t: (s, t, 0))
    return pl.pallas_call(
        functools.partial(_final_kernel, alpha=alpha),
        grid=(rows // tm,),
        in_specs=[tile(d), yspec(0), yspec(1), tile(LANES), _mod_spec(tiles_per_group, None),
                  _const_spec((1, d)), _const_spec((1, d))],
        out_specs=tile(d),
        out_shape=jax.ShapeDtypeStruct((rows, d), F32),
        compiler_params=_cparams(("parallel",)),
        name="moe_combine_norm",
    )(x1, y2, y2, route, mod, ln_g, ln_b)


def _block_diag4(m):
    z = jnp.zeros_like(m[0])
    return jnp.concatenate(
        [jnp.concatenate([m[g] if gg == g else z for gg in range(4)], axis=1) for g in range(4)], axis=0)


def _rope_tables(n):
    t = np.arange(n)
    inv_freq = jnp.power(ROPE_THETA, -jnp.arange(ROPE_FREQS, dtype=F32) / ROPE_FREQS)
    row = jnp.asarray(t // GRID_W, F32)[:, None] * inv_freq
    col = jnp.asarray(t % GRID_W, F32)[:, None] * inv_freq
    cos = jnp.concatenate([jnp.cos(row)] * 2 + [jnp.cos(col)] * 2, axis=1)
    sin = jnp.concatenate([-jnp.sin(row), jnp.sin(row), -jnp.sin(col), jnp.sin(col)], axis=1)
    return jnp.tile(cos, (1, 256 // HEAD_DIM)), jnp.tile(sin, (1, 256 // HEAD_DIM))


def _dft_tables(n):
    k = jnp.arange(n, dtype=jnp.int32)
    ang = ((k[:, None] * k[None, :]) % n).astype(F32) * (2.0 * math.pi / n)
    return jnp.cos(ang), jnp.sin(ang)


def _gqa_perm():
    perm = np.zeros(GQA_Q, np.int64)
    for j in range(GQA_GROUP):
        for g in range(GQA_KV_HEADS):
            dst = 256 * j + HEAD_DIM * g
            srcc = (GQA_GROUP * g + j) * HEAD_DIM
            perm[dst:dst + HEAD_DIM] = np.arange(srcc, srcc + HEAD_DIM)
    return perm


def kernel(x, c, ctx, c_ctx, ada_w, ada_b, ln_mix_g, ln_mix_b, ln_ffn_g, ln_ffn_b, w_out, ev_w_in, ev_pool_w, ev_pool_scale, ev_q_gain, ev_k_gain, ev_ffn_w1, ev_ffn_w3, ev_ffn_w2, od_w_in, od_fourier_gain, od_rpb, od_router, od_exp_w1, od_exp_w3, od_exp_w2):
    nb, seq, d = x.shape
    lc = ctx.shape[1]
    depth = ada_w.shape[0]
    assert d == D_MODEL and nb + 1 <= MOD_ROWS and seq % (NA_KROWS * GRID_W) == 0
    n_tok, n_ctx = nb * seq, nb * lc
    tm = 512
    tpg = seq // tm
    tm_c = min(tm, lc)
    alpha = float((2 * depth) ** 0.25)

    c_all = jnp.concatenate([c, c_ctx[None, :], jnp.zeros((MOD_ROWS - nb - 1, d), F32)], axis=0)
    mod_all = _modulation(c_all, ada_w, ada_b)

    gmat = _block_diag4(jnp.full((4, HEAD_DIM, HEAD_DIM), 1.0 / HEAD_DIM, BF16))
    tile4 = lambda g: jnp.tile(g.reshape(1, HEAD_DIM), (1, 256 // HEAD_DIM))
    row = lambda v: v.reshape(1, -1)

    xf = x.reshape(n_tok, d)
    cf = ctx.reshape(n_ctx, d)
    lat = dict(tm=tm, tiles_per_group=tpg, fixed_row=None)
    cx = dict(tm=tm_c, tiles_per_group=1, fixed_row=nb)

    for layer in range(depth):
        with_ctx = layer < depth - 1
        i = layer // 2
        mod = mod_all[layer]
        lg1, lb1 = row(ln_mix_g[layer]), row(ln_mix_b[layer])
        lg2, lb2 = row(ln_ffn_g[layer]), row(ln_ffn_b[layer])
        wo = w_out[layer].astype(BF16)
        if layer % 2 == 0:
            perm = _gqa_perm()
            w_in = ev_w_in[i]
            w_in = jnp.concatenate(
                [w_in[:, :POOL_WIDTH], w_in[:, POOL_WIDTH:POOL_WIDTH + GQA_Q][:, perm],
                 w_in[:, POOL_WIDTH + GQA_Q:]], axis=1).astype(BF16)
            wa, wb = wo[:POOL_WIDTH], wo[POOL_WIDTH:][perm]
            qg, kg = tile4(ev_q_gain[i]), tile4(ev_k_gain[i])
            cos, sin = _rope_tables(seq)
            pw = _block_diag4(ev_pool_w[i].astype(BF16))
            ps = row(ev_pool_scale[i])
            w1, w3, w2 = (ev_ffn_w1[i].astype(BF16), ev_ffn_w3[i].astype(BF16), ev_ffn_w2[i].astype(BF16))

            a, q, k, v = _inproj_even(xf, mod, w_in, gmat, qg, kg, cos, sin, rope=True, **lat)
            a_c, q_c, k_c, v_c = _inproj_even(cf, mod, w_in, gmat, qg, kg, cos, sin, rope=False, **cx)
            o = _gqa_attention(q, k, v, k_c, v_c, tq=tm, n=seq, lc=lc)
            ya = _pool(a, pw, ps, n=seq)
            x1, h2 = _outproj(xf, ya, o, mod, wa, wb, lg1, lb1, None, alpha=alpha, **lat)
            xf = _ffn(h2, x1, mod, w1, w3, w2, lg2, lb2, alpha=alpha, **lat)
            if with_ctx:
                o_c = _gqa_attention(q_c, None, None, k_c, v_c, tq=lc, n=seq, lc=lc)
                ya_c = _pool(a_c, pw, ps, n=lc)
                x1_c, h2_c = _outproj(cf, ya_c, o_c, mod, wa, wb, lg1, lb1, None, alpha=alpha, **cx)
                cf = _ffn(h2_c, x1_c, mod, w1, w3, w2, lg2, lb2, alpha=alpha, **cx)
        else:
            assert not with_ctx, "context outputs of an odd layer are not produced"
            w_in = od_w_in[i].astype(BF16)
            wa, wb = wo[:FOURIER_WIDTH], wo[FOURIER_WIDTH:]
            fg = row(od_fourier_gain[i])
            ch = np.arange(HEAD_DIM)
            ang = 2.0 * np.pi * ((ch[:, None] * ch[None, :]) % HEAD_DIM) / HEAD_DIM
            cc = _block_diag4(jnp.asarray(np.stack([np.cos(ang)] * 4), BF16))
            sc = _block_diag4(jnp.asarray(np.stack([np.sin(ang)] * 4), BF16))
            cn, sn = _dft_tables(seq)
            variants, tab = _na_tables(seq // GRID_W)
            bias = _na_bias(od_rpb[i], variants)
            w_router = jnp.pad(od_router[i], ((0, 0), (0, LANES - N_EXPERTS)))
            w1, w3, w2 = (od_exp_w1[i].astype(BF16), od_exp_w3[i].astype(BF16), od_exp_w2[i].astype(BF16))

            fa, fb, q, k, v = _inproj_odd(xf, mod, w_in, gmat, fg, cc, sc, **lat)
            _, _, _, k_c, v_c = _inproj_odd(cf, mod, w_in, gmat, fg, cc, sc, **cx)
            ya = _dft_positions(cn.astype(BF16), (-sn).astype(BF16), fa, fb, n=seq, tm=tm)
            o = _na_attention(q, k, v, k_c, v_c, bias, jnp.asarray(tab), n=seq, lc=lc)
            x1, h2, route = _outproj(xf, ya, o, mod, wa, wb, lg1, lb1, w_router, alpha=alpha, **lat)
            idx = route[:, :2].astype(jnp.int32)
            te, tu, src, dst = _route_tables(idx, n_tok, MOE_TILE)
            y2 = _moe_experts(h2, te, tu, src, dst, w1, w3, w2, out_rows=2 * (n_tok + MOE_TILE))
            y2 = y2.reshape(2, n_tok + MOE_TILE, d)
            xf = _moe_combine(x1, y2, route, mod, lg2, lb2, alpha=alpha, tm=tm, tiles_per_group=tpg)
    return xf.reshape(nb, seq, d)
```

```python
import functools
import math

import numpy as np
import jax
import jax.numpy as jnp
from jax import lax
from jax.experimental import pallas as pl
from jax.experimental.pallas import tpu as pltpu

F32 = jnp.float32
BF16 = jnp.bfloat16

D_MODEL = 1024
HEAD_DIM = 64
GRID_W = 64
POOL_WIDTH = 256
POOL_WINDOWS = (2, 4, 8, 16)
GQA_HEADS = 12
GQA_KV_HEADS = 4
GQA_GROUP = 3
GQA_Q = 768
GQA_KV = 256
FOURIER_WIDTH = 256
NA_HEADS = 12
NA_WIDTH = 768
MAX_WIN_R = 8
WIN_C = 16
ROPE_THETA = 10000.0
ROPE_FREQS = 16
FFN_DIM = 2816
N_EXPERTS = 8
LN_EPS = 1e-6
RMS_EPS = 1e-6
LOG2E = math.log2(math.e)
QK_SCALE = HEAD_DIM ** -0.5 * LOG2E

LANES = 128
MOD_ROWS = 16
VMEM_LIMIT = 56 * 1024 * 1024
NA_QROWS = 8
NA_KROWS = 16
MOE_TILE = 256


def _cparams(sem):
    return pltpu.CompilerParams(dimension_semantics=sem, vmem_limit_bytes=VMEM_LIMIT)


def _const_spec(shape):
    nd = len(shape)
    return pl.BlockSpec(shape, lambda *_: (0,) * nd, pipeline_mode=pl.Buffered(1))


def _layer_norm(v, g, b):
    mu = jnp.mean(v, axis=-1, keepdims=True)
    d = v - mu
    var = jnp.mean(d * d, axis=-1, keepdims=True)
    return d * lax.rsqrt(var + LN_EPS) * g + b


def _dot(a, b):
    return jnp.dot(a, b, preferred_element_type=F32)


def _dot_nt(a, b):
    return lax.dot_general(a, b, (((1,), (1,)), ((), ())), preferred_element_type=F32)


def _lane_partial_sum(p):
    acc = p[:, :LANES]
    for j in range(1, p.shape[1] // LANES):
        acc = acc + p[:, LANES * j:LANES * (j + 1)]
    return acc


def _lane_partial_max(s):
    acc = s[:, :LANES]
    for j in range(1, s.shape[1] // LANES):
        acc = jnp.maximum(acc, s[:, LANES * j:LANES * (j + 1)])
    return acc


def _head_slot(shape):
    return lax.broadcasted_iota(jnp.int32, shape, len(shape) - 1) // HEAD_DIM


def _mod_kernel(c_ref, w_ref, b_ref, o_ref):
    c = c_ref[...]
    s = c * jax.nn.sigmoid(c)
    o_ref[...] = jnp.dot(s, w_ref[...], preferred_element_type=F32,
                         precision=lax.Precision.HIGHEST) + b_ref[...]


def _modulation(c_all, ada_w, ada_b):
    depth = ada_w.shape[0]
    d = D_MODEL
    out = pl.pallas_call(
        _mod_kernel,
        grid=(depth, 6),
        in_specs=[
            pl.BlockSpec((MOD_ROWS, d), lambda l, n: (0, 0)),
            pl.BlockSpec((None, d, d), lambda l, n: (l, 0, n)),
            pl.BlockSpec((None, 1, d), lambda l, n: (l, 0, n)),
        ],
        out_specs=pl.BlockSpec((None, MOD_ROWS, d), lambda l, n: (l, 0, n)),
        out_shape=jax.ShapeDtypeStruct((depth, MOD_ROWS, 6 * d), F32),
        compiler_params=_cparams(("arbitrary", "arbitrary")),
        name="adaln_mod",
    )(c_all, ada_w, ada_b.reshape(depth, 1, 6 * d))
    return out.reshape(depth, MOD_ROWS, 6, d)


def _mod_spec(tiles_per_group, fixed_row):
    if fixed_row is None:
        return pl.BlockSpec((1, 6, D_MODEL), lambda t: (t // tiles_per_group, 0, 0))
    return pl.BlockSpec((1, 6, D_MODEL), lambda t: (fixed_row, 0, 0))


def _group_norm(u, gmat, gain):
    ms = _dot((u * u).astype(BF16), gmat)
    return u * lax.rsqrt(ms + RMS_EPS) * gain


def _rope(u, cos, sin):
    lane = lax.broadcasted_iota(jnp.int32, u.shape, 1)
    first = (lane // ROPE_FREQS) % 2 == 0
    n = u.shape[1]
    partner = jnp.where(first, pltpu.roll(u, n - ROPE_FREQS, 1), pltpu.roll(u, ROPE_FREQS, 1))
    return u * cos + partner * sin


def _inproj_even_kernel(x_ref, mod_ref, w_ref, g_ref, qg_ref, kg_ref, cos_ref, sin_ref,
                        a_ref, q_ref, k_ref, v_ref, *, rope):
    x = x_ref[...]
    h = (x * (1.0 + mod_ref[0, 1:2, :]) + mod_ref[0, 0:1, :]).astype(BF16)
    p = _dot(h, w_ref[...])
    a_ref[...] = p[:, :POOL_WIDTH]
    gmat = g_ref[...]
    if rope:
        cos = cos_ref[...]
        sin = sin_ref[...]
    for c in range(GQA_Q // 256):
        u = _group_norm(p[:, POOL_WIDTH + 256 * c:POOL_WIDTH + 256 * (c + 1)], gmat, qg_ref[...])
        if rope:
            u = _rope(u, cos, sin)
        q_ref[:, 256 * c:256 * (c + 1)] = (u * QK_SCALE).astype(BF16)
    u = _group_norm(p[:, POOL_WIDTH + GQA_Q:POOL_WIDTH + GQA_Q + GQA_KV], gmat, kg_ref[...])
    if rope:
        u = _rope(u, cos, sin)
    k_ref[...] = u.astype(BF16)
    v_ref[...] = p[:, POOL_WIDTH + GQA_Q + GQA_KV:].astype(BF16)


def _inproj_even(xf, mod, w_in, gmat, qg, kg, cos, sin, *, tm, tiles_per_group, fixed_row, rope):
    rows = xf.shape[0]
    d = D_MODEL
    n_in = w_in.shape[1]
    tile = lambda w: pl.BlockSpec((tm, w), lambda t: (t, 0))
    if rope:
        tab = pl.BlockSpec((tm, 256), lambda t: (t % tiles_per_group, 0))
    else:
        tab = pl.BlockSpec((tm, 256), lambda t: (0, 0))
    return pl.pallas_call(
        functools.partial(_inproj_even_kernel, rope=rope),
        grid=(rows // tm,),
        in_specs=[tile(d), _mod_spec(tiles_per_group, fixed_row), _const_spec((d, n_in)),
                  _const_spec((256, 256)), _const_spec((1, 256)), _const_spec((1, 256)), tab, tab],
        out_specs=[tile(POOL_WIDTH), tile(GQA_Q), tile(GQA_KV), tile(GQA_KV)],
        out_shape=[jax.ShapeDtypeStruct((rows, POOL_WIDTH), F32),
                   jax.ShapeDtypeStruct((rows, GQA_Q), BF16),
                   jax.ShapeDtypeStruct((rows, GQA_KV), BF16),
                   jax.ShapeDtypeStruct((rows, GQA_KV), BF16)],
        compiler_params=_cparams(("parallel",)),
        name="inproj_even",
    )(xf, mod, w_in, gmat, qg, kg, cos, sin)


GQA_KCHUNK = 1024


def _gqa_kernel(*refs, n_lat):
    if n_lat:
        q_ref, k_ref, v_ref, kc_ref, vc_ref, o_ref = refs
    else:
        q_ref, kc_ref, vc_ref, o_ref = refs
    q = q_ref[...]
    slot = _head_slot(q.shape)
    kc = kc_ref[...]
    vc = vc_ref[...]

    def head(g, acc):
        qm = jnp.where(slot == g, q, jnp.zeros_like(q))
        s = _dot_nt(qm, kc)
        m = s.max(axis=-1, keepdims=True)
        p = jnp.exp2(s - m)
        l = _lane_partial_sum(p)
        o = _dot(p.astype(BF16), vc)

        def chunk(ci, carry):
            m, l, o = carry
            start = pl.multiple_of(ci * GQA_KCHUNK, GQA_KCHUNK)
            s = _dot_nt(qm, k_ref[pl.ds(start, GQA_KCHUNK), :])
            m_new = jnp.maximum(m, s.max(axis=-1, keepdims=True))
            corr = jnp.exp2(m - m_new)
            p = jnp.exp2(s - m_new)
            l = l * corr + _lane_partial_sum(p)
            o = o * corr + _dot(p.astype(BF16), v_ref[pl.ds(start, GQA_KCHUNK), :])
            return m_new, l, o

        if n_lat:
            m, l, o = lax.fori_loop(0, n_lat // GQA_KCHUNK, chunk, (m, l, o), unroll=True)
        return jnp.where(slot == g, o / l.sum(axis=-1, keepdims=True), acc)

    acc = lax.fori_loop(0, GQA_KV_HEADS, head, jnp.zeros(q.shape, F32), unroll=True)
    o_ref[...] = acc.astype(BF16)


def _gqa_attention(q, k, v, kc, vc, *, tq, n, lc):
    rows = q.shape[0]
    q_per_batch = n if k is not None else lc
    nq = q_per_batch // tq
    qspec = pl.BlockSpec((tq, 256), lambda b, i, c: (b * nq + i, c))
    cspec = pl.BlockSpec((lc, GQA_KV), lambda b, i, c: (b, 0))
    in_specs, args = [qspec], [q]
    if k is not None:
        lspec = pl.BlockSpec((n, GQA_KV), lambda b, i, c: (b, 0))
        in_specs += [lspec, lspec]
        args += [k, v]
    in_specs += [cspec, cspec]
    args += [kc, vc]
    return pl.pallas_call(
        functools.partial(_gqa_kernel, n_lat=n if k is not None else 0),
        grid=(rows // q_per_batch, nq, GQA_Q // 256),
        in_specs=in_specs,
        out_specs=qspec,
        out_shape=jax.ShapeDtypeStruct((rows, GQA_Q), BF16),
        compiler_params=_cparams(("parallel", "arbitrary", "arbitrary")),
        name="gqa_attention",
    )(*args)


POOL_CHUNK = 512
POOL_HALO = 8


def _pool_kernel(a_ref, pw_ref, ps_ref, o_ref, ext_ref, *, n):
    chunk = min(POOL_CHUNK, n)
    zeros = jnp.zeros((POOL_HALO, POOL_WIDTH), F32)
    ext_ref[0:POOL_HALO, :] = zeros
    ext_ref[n + POOL_HALO:n + 2 * POOL_HALO, :] = zeros
    ext_ref[POOL_HALO:n + POOL_HALO, :] = a_ref[...]
    ln = chunk + 2 * POOL_HALO
    lane_grp = _head_slot((chunk, POOL_WIDTH))
    win = jnp.where(lane_grp == 0, POOL_WINDOWS[0],
                    jnp.where(lane_grp == 1, POOL_WINDOWS[1],
                              jnp.where(lane_grp == 2, POOL_WINDOWS[2], POOL_WINDOWS[3])))
    pw = pw_ref[...]
    ps = ps_ref[...]

    def body(ci, carry):
        start = pl.multiple_of(ci * chunk, chunk)
        e = ext_ref[pl.ds(start, ln), :]
        s2 = e + pltpu.roll(e, 1, 0)
        s4 = pltpu.roll(s2, 1, 0) + pltpu.roll(s2, ln - 1, 0)
        s8 = pltpu.roll(s4, 2, 0) + pltpu.roll(s4, ln - 2, 0)
        s16 = pltpu.roll(s8, 4, 0) + pltpu.roll(s8, ln - 4, 0)
        cut = lambda z: z[POOL_HALO:POOL_HALO + chunk, :]
        tot = jnp.where(lane_grp == 0, cut(s2),
                        jnp.where(lane_grp == 1, cut(s4),
                                  jnp.where(lane_grp == 2, cut(s8), cut(s16))))
        t = lax.broadcasted_iota(jnp.int32, (chunk, POOL_WIDTH), 0) + start
        lo = jnp.clip(t - win // 2, 0, n)
        hi = jnp.clip(t - win // 2 + win, 0, n)
        mean = tot / (hi - lo).astype(F32)
        m = mean - cut(e)
        o_ref[pl.ds(start, chunk), :] = (_dot(m.astype(BF16), pw) * ps).astype(BF16)
        return carry

    lax.fori_loop(0, n // chunk, body, 0)


def _pool(a, pw_bd, ps, *, n):
    rows = a.shape[0]
    return pl.pallas_call(
        functools.partial(_pool_kernel, n=n),
        grid=(rows // n,),
        in_specs=[pl.BlockSpec((n, POOL_WIDTH), lambda b: (b, 0)),
                  _const_spec((POOL_WIDTH, POOL_WIDTH)), _const_spec((1, POOL_WIDTH))],
        out_specs=pl.BlockSpec((n, POOL_WIDTH), lambda b: (b, 0)),
        out_shape=jax.ShapeDtypeStruct((rows, POOL_WIDTH), BF16),
        scratch_shapes=[pltpu.VMEM((n + 2 * POOL_HALO, POOL_WIDTH), F32)],
        compiler_params=_cparams(("parallel",)),
        name="multiscale_pool",
    )(a, pw_bd, ps)


def _outproj(xf, ya, yb, mod, wa, wb, ln_g, ln_b, w_router, *, alpha, tm, tiles_per_group, fixed_row):
    rows = xf.shape[0]
    d = D_MODEL
    router = w_router is not None
    tile = lambda w: pl.BlockSpec((tm, w), lambda t: (t, 0))
    in_specs = [tile(d), tile(ya.shape[1]), tile(yb.shape[1]), _mod_spec(tiles_per_group, fixed_row),
                _const_spec(wa.shape), _const_spec(wb.shape), _const_spec((1, d)), _const_spec((1, d))]
    args = [xf, ya, yb, mod, wa, wb, ln_g, ln_b]
    out_specs = [tile(d), tile(d)]
    out_shape = [jax.ShapeDtypeStruct((rows, d), F32),
                 jax.ShapeDtypeStruct((rows, d), F32 if router else BF16)]
    if router:
        wr_hi = w_router.astype(BF16)
        wr_lo = (w_router - wr_hi.astype(F32)).astype(BF16)
        in_specs += [_const_spec(w_router.shape)] * 2
        args += [wr_hi, wr_lo]
        out_specs.append(tile(LANES))
        out_shape.append(jax.ShapeDtypeStruct((rows, LANES), F32))
    return pl.pallas_call(
        functools.partial(_outproj_body, router=router, alpha=alpha),
        grid=(rows // tm,),
        in_specs=in_specs,
        out_specs=out_specs,
        out_shape=out_shape,
        compiler_params=_cparams(("parallel",)),
        name="outproj_norm",
    )(*args)


def _outproj_body(*refs, router, alpha):
    x_ref, ya_ref, yb_ref, mod_ref, wa_ref, wb_ref, g_ref, b_ref = refs[:8]
    if router:
        wrh_ref, wrl_ref, x1_ref, h2_ref, route_ref = refs[8:]
    else:
        x1_ref, h2_ref = refs[8:]
    y = _dot(ya_ref[...], wa_ref[...]) + _dot(yb_ref[...], wb_ref[...])
    x1 = _layer_norm(alpha * x_ref[...] + mod_ref[0, 2:3, :] * y, g_ref[...], b_ref[...])
    x1_ref[...] = x1
    h2 = x1 * (1.0 + mod_ref[0, 4:5, :]) + mod_ref[0, 3:4, :]
    h2_ref[...] = h2.astype(h2_ref.dtype)
    if router:
        xh = h2.astype(BF16)
        xl = (h2 - xh.astype(F32)).astype(BF16)
        logits = _dot(xh, wrh_ref[...]) + (_dot(xh, wrl_ref[...]) + _dot(xl, wrh_ref[...]))
        lane = lax.broadcasted_iota(jnp.int32, logits.shape, 1)
        neg = jnp.float32(-jnp.inf)
        lg = jnp.where(lane < N_EXPERTS, logits, neg)
        m1 = lg.max(axis=-1, keepdims=True)
        i1 = jnp.where(lg == m1, lane, LANES).min(axis=-1, keepdims=True)
        lg2 = jnp.where(lane == i1, neg, lg)
        m2 = lg2.max(axis=-1, keepdims=True)
        i2 = jnp.where(lg2 == m2, lane, LANES).min(axis=-1, keepdims=True)
        e = jnp.exp(m2 - m1)
        w1 = 1.0 / (1.0 + e)
        w2 = e / (1.0 + e)
        route_ref[...] = jnp.where(lane == 0, i1.astype(F32),
                                   jnp.where(lane == 1, i2.astype(F32),
                                             jnp.where(lane == 2, w1,
                                                       jnp.where(lane == 3, w2, 0.0))))


def _ffn_kernel(h_ref, x1_ref, mod_ref, w1_ref, w3_ref, w2_ref, g_ref, b_ref, o_ref, *, alpha):
    h = h_ref[...]
    g = _dot(h, w1_ref[...])
    u = _dot(h, w3_ref[...])
    act = (g * jax.nn.sigmoid(g) * u).astype(BF16)
    f = _dot(act, w2_ref[...])
    o_ref[...] = _layer_norm(alpha * x1_ref[...] + mod_ref[0, 5:6, :] * f, g_ref[...], b_ref[...])


def _ffn(h2, x1, mod, w1, w3, w2, ln_g, ln_b, *, alpha, tm, tiles_per_group, fixed_row):
    rows = x1.shape[0]
    d = D_MODEL
    tile = lambda w: pl.BlockSpec((tm, w), lambda t: (t, 0))
    return pl.pallas_call(
        functools.partial(_ffn_kernel, alpha=alpha),
        grid=(rows // tm,),
        in_specs=[tile(d), tile(d), _mod_spec(tiles_per_group, fixed_row),
                  _const_spec(w1.shape), _const_spec(w3.shape), _const_spec(w2.shape),
                  _const_spec((1, d)), _const_spec((1, d))],
        out_specs=tile(d),
        out_shape=jax.ShapeDtypeStruct((rows, d), F32),
        compiler_params=_cparams(("parallel",)),
        name="swiglu_norm",
    )(h2, x1, mod, w1, w3, w2, ln_g, ln_b)


def _inproj_odd_kernel(x_ref, mod_ref, w_ref, g_ref, fg_ref, cc_ref, sc_ref,
                       fa_ref, fb_ref, q_ref, k_ref, v_ref):
    x = x_ref[...]
    h = (x * (1.0 + mod_ref[0, 1:2, :]) + mod_ref[0, 0:1, :]).astype(BF16)
    p = _dot(h, w_ref[...])
    f = _group_norm(p[:, :FOURIER_WIDTH], g_ref[...], fg_ref[...]).astype(BF16)
    fa_ref[...] = _dot(f, cc_ref[...]).astype(BF16)
    fb_ref[...] = _dot(f, sc_ref[...]).astype(BF16)
    o = FOURIER_WIDTH
    q_ref[...] = (p[:, o:o + NA_WIDTH] * QK_SCALE).astype(BF16)
    k_ref[...] = p[:, o + NA_WIDTH:o + 2 * NA_WIDTH].astype(BF16)
    v_ref[...] = p[:, o + 2 * NA_WIDTH:].astype(BF16)


def _inproj_odd(xf, mod, w_in, gmat, fg, cc, sc, *, tm, tiles_per_group, fixed_row):
    rows = xf.shape[0]
    d = D_MODEL
    tile = lambda w: pl.BlockSpec((tm, w), lambda t: (t, 0))
    widths = (FOURIER_WIDTH, FOURIER_WIDTH, NA_WIDTH, NA_WIDTH, NA_WIDTH)
    return pl.pallas_call(
        _inproj_odd_kernel,
        grid=(rows // tm,),
        in_specs=[tile(d), _mod_spec(tiles_per_group, fixed_row), _const_spec(w_in.shape),
                  _const_spec((256, 256)), _const_spec((1, 256)),
                  _const_spec((256, 256)), _const_spec((256, 256))],
        out_specs=[tile(w) for w in widths],
        out_shape=[jax.ShapeDtypeStruct((rows, w), BF16) for w in widths],
        compiler_params=_cparams(("parallel",)),
        name="inproj_odd",
    )(xf, mod, w_in, gmat, fg, cc, sc)


def _dft_kernel(c2_ref, s2_ref, c1_ref, s1_ref, a_ref, b_ref, o_ref, cn_ref, sn_ref, *, scale):
    @pl.when(pl.program_id(1) == 0)
    def _():
        c1, s1 = c1_ref[0], s1_ref[0]
        c2, s2 = c2_ref[...], s2_ref[...]
        cn_ref[...] = (c2 * c1 - s2 * s1).astype(BF16)
        sn_ref[...] = (-(s2 * c1 + c2 * s1)).astype(BF16)

    o = _dot(cn_ref[...], a_ref[...]) + _dot(sn_ref[...], b_ref[...])
    o_ref[...] = (o * scale).astype(BF16)


def _dft_positions(c2, s2, c1, s1, fa, fb, *, n, tm):
    rows = fa.shape[0]
    nm = n // tm
    blk = pl.BlockSpec((1, 1, n), lambda m, b: (m, 0, 0))
    return pl.pallas_call(
        functools.partial(_dft_kernel, scale=float((n * HEAD_DIM) ** -0.5)),
        grid=(nm, rows // n),
        in_specs=[_const_spec((tm, n)), _const_spec((tm, n)), blk, blk,
                  pl.BlockSpec((n, FOURIER_WIDTH), lambda m, b: (b, 0)),
                  pl.BlockSpec((n, FOURIER_WIDTH), lambda m, b: (b, 0))],
        out_specs=pl.BlockSpec((tm, FOURIER_WIDTH), lambda m, b: (b * nm + m, 0)),
        out_shape=jax.ShapeDtypeStruct((rows, FOURIER_WIDTH), BF16),
        scratch_shapes=[pltpu.VMEM((tm, n), BF16), pltpu.VMEM((tm, n), BF16)],
        compiler_params=_cparams(("arbitrary", "arbitrary")),
        name="dft_positions",
    )(c2, s2, c1, s1, fa, fb)


NA_KPIECE = 256
NA_NPIECE = NA_KROWS * GRID_W // NA_KPIECE


def _na_kernel(tab_ref, *refs):
    del tab_ref
    q_ref = refs[0]
    k_refs = refs[1:1 + NA_NPIECE]
    v_refs = refs[1 + NA_NPIECE:1 + 2 * NA_NPIECE]
    kc_ref, vc_ref, bias_ref, o_ref = refs[1 + 2 * NA_NPIECE:]
    q = q_ref[...]
    slot = _head_slot(q.shape)
    kall = jnp.concatenate([r[...] for r in k_refs] + [kc_ref[...]], axis=0)
    vall = jnp.concatenate([r[...] for r in v_refs] + [vc_ref[...]], axis=0)
    n_loc = NA_NPIECE * NA_KPIECE

    def head(hh, acc):
        qm = jnp.where(slot == hh, q, jnp.zeros_like(q))
        s = _dot_nt(qm, kall)
        s_loc = s[:, :n_loc] + bias_ref[hh]
        s_ctx = s[:, n_loc:]
        m = jnp.maximum(_lane_partial_max(s_loc), _lane_partial_max(s_ctx)).max(axis=-1, keepdims=True)
        p_loc = jnp.exp2(s_loc - m)
        p_ctx = jnp.exp2(s_ctx - m)
        l = _lane_partial_sum(p_loc) + _lane_partial_sum(p_ctx)
        p = jnp.concatenate([p_loc.astype(BF16), p_ctx.astype(BF16)], axis=1)
        o = _dot(p, vall)
        return jnp.where(slot == hh, o / l.sum(axis=-1, keepdims=True), acc)

    acc = lax.fori_loop(0, 256 // HEAD_DIM, head, jnp.zeros(q.shape, F32), unroll=True)
    o_ref[...] = acc.astype(BF16)


def _na_tables(rows):
    nj = rows // NA_QROWS
    variants, var_of_j, kb_of_j = [], [], []
    for j in range(nj):
        ks = min(max(NA_QROWS * j - MAX_WIN_R // 2, 0), rows - NA_KROWS)
        d = ks - NA_QROWS * j + MAX_WIN_R - 1
        ls = tuple(min(max(NA_QROWS * j + i - MAX_WIN_R // 2, 0), rows - MAX_WIN_R) - ks
                   for i in range(NA_QROWS))
        key = (d, ls)
        if key not in variants:
            variants.append(key)
        var_of_j.append(variants.index(key))
        kb_of_j.append(ks * GRID_W // NA_KPIECE)
    return variants, np.asarray([var_of_j, kb_of_j], np.int32)


def _na_bias(rpb, variants):
    nh = rpb.shape[0]
    qc = np.arange(GRID_W)[:, None]
    kc = np.arange(GRID_W)[None, :]
    cs = np.clip(qc - WIN_C // 2, 0, GRID_W - WIN_C)
    col_ok = (kc >= cs) & (kc < cs + WIN_C)
    col_idx = np.clip(kc - qc + WIN_C - 1, 0, 2 * WIN_C - 2)
    col_sel = (col_idx[..., None] == np.arange(2 * WIN_C - 1)) & col_ok[..., None]
    tcol = jnp.einsum("hab,qkb->haqk", rpb.astype(F32), jnp.asarray(col_sel, F32),
                      precision=lax.Precision.HIGHEST)
    i = np.arange(NA_QROWS)[:, None]
    r = np.arange(NA_KROWS)[None, :]
    row_sel, ok = [], []
    for d, ls in variants:
        lsa = np.asarray(ls)[:, None]
        row_ok = (r >= lsa) & (r < lsa + MAX_WIN_R)
        a = np.clip(r - i + d, 0, 2 * MAX_WIN_R - 2)
        row_sel.append((a[..., None] == np.arange(2 * MAX_WIN_R - 1)) & row_ok[..., None])
        ok.append(row_ok[:, None, :, None] & col_ok[None, :, None, :])
    row_sel = jnp.asarray(np.stack(row_sel), F32)
    ok = jnp.asarray(np.stack(ok))
    bias = jnp.einsum("vira,haqk->vhiqrk", row_sel, tcol, precision=lax.Precision.HIGHEST)
    bias = jnp.where(ok[:, None], bias * LOG2E, -jnp.inf)
    nq = NA_QROWS * GRID_W
    return bias.reshape(len(variants), nh, nq, NA_KROWS * GRID_W)


def _na_attention(q, k, v, kc, vc, bias, tab, *, n, lc):
    rows = q.shape[0]
    nb = rows // n
    nq = NA_QROWS * GRID_W
    nj = n // nq
    kblocks = n // NA_KPIECE
    nchunk = NA_WIDTH // 256
    qspec = pl.BlockSpec((nq, 256), lambda j, c, b, tab: (b * nj + j, c))

    def piece(i):
        return pl.BlockSpec((NA_KPIECE, 256), lambda j, c, b, tab: (b * kblocks + tab[1, j] + i, c))

    cspec = pl.BlockSpec((lc, 256), lambda j, c, b, tab: (b, c))
    bspec = pl.BlockSpec((None, 256 // HEAD_DIM, nq, NA_KROWS * GRID_W),
                         lambda j, c, b, tab: (tab[0, j], c, 0, 0))
    grid_spec = pltpu.PrefetchScalarGridSpec(
        num_scalar_prefetch=1,
        grid=(nj, nchunk, nb),
        in_specs=[qspec] + [piece(i) for i in range(NA_NPIECE)] * 2 + [cspec, cspec, bspec],
        out_specs=qspec,
    )
    return pl.pallas_call(
        _na_kernel,
        grid_spec=grid_spec,
        out_shape=jax.ShapeDtypeStruct((rows, NA_WIDTH), BF16),
        compiler_params=_cparams(("arbitrary", "arbitrary", "arbitrary")),
        name="neighbourhood_attention",
    )(tab, q, *([k] * NA_NPIECE), *([v] * NA_NPIECE), kc, vc, bias)


def _moe_kernel(te_ref, p0_ref, pn_ref, pp_ref, h_hbm, w1_ref, w3_ref, w2_ref, y_hbm,
                xbuf, ybuf, gsem, ssem, *, n_tok):
    del te_ref
    i = pl.program_id(0)
    last = pl.num_programs(0) - 1
    tr = xbuf.shape[1]
    slot = i % 2
    other = 1 - slot

    def gather(pairs, buf, r):
        tok = jnp.minimum(pairs[0, r] >> 1, n_tok - 1)
        return pltpu.make_async_copy(h_hbm.at[pl.ds(tok, 1), :], xbuf.at[buf, pl.ds(r, 1), :], gsem.at[buf])

    def scatter(pairs, buf, r):
        pair = pairs[0, r]
        dst = (pair & 1) * n_tok + (pair >> 1)
        return pltpu.make_async_copy(ybuf.at[buf, pl.ds(r, 1), :], y_hbm.at[pl.ds(dst, 1), :], ssem)

    def each_row(fn):
        for r in range(tr):
            fn(r)

    @pl.when(i == 0)
    def _():
        ybuf[...] = jnp.zeros(ybuf.shape, F32)
        each_row(lambda r: gather(p0_ref, 0, r).start())

    each_row(lambda r: gather(pn_ref, slot, r).wait())
    each_row(lambda r: gather(pn_ref, other, r).start())
    each_row(lambda r: scatter(pp_ref, other, r).start())

    x = xbuf[slot].astype(BF16)
    g = _dot(x, w1_ref[...])
    u = _dot(x, w3_ref[...])
    act = (g * jax.nn.sigmoid(g) * u).astype(BF16)
    ybuf[slot] = _dot(act, w2_ref[...])

    each_row(lambda r: scatter(pp_ref, other, r).wait())

    @pl.when(i == last)
    def _():
        each_row(lambda r: gather(pn_ref, other, r).wait())


def _moe_experts(h2, te, pairs, w1, w3, w2, *, n_tok):
    d = D_MODEL
    n_steps, tr = pairs.shape[0] - 2, pairs.shape[2]
    wspec = lambda shp: pl.BlockSpec((None,) + shp, lambda i, te: (te[i], 0, 0))
    ptile = lambda off: pl.BlockSpec((None, 1, tr), off, memory_space=pltpu.SMEM)
    first = ptile(lambda i, te: (1, 0, 0))
    nxt = ptile(lambda i, te: (i + 2, 0, 0))
    prv = ptile(lambda i, te: (i, 0, 0))
    grid_spec = pltpu.PrefetchScalarGridSpec(
        num_scalar_prefetch=1,
        grid=(n_steps,),
        in_specs=[first, nxt, prv, pl.BlockSpec(memory_space=pl.ANY),
                  wspec((d, FFN_DIM)), wspec((d, FFN_DIM)), wspec((FFN_DIM, d))],
        out_specs=pl.BlockSpec(memory_space=pl.ANY),
        scratch_shapes=[pltpu.VMEM((2, tr, d), F32), pltpu.VMEM((2, tr, d), F32),
                        pltpu.SemaphoreType.DMA((2,)), pltpu.SemaphoreType.DMA(())],
    )
    return pl.pallas_call(
        functools.partial(_moe_kernel, n_tok=n_tok),
        grid_spec=grid_spec,
        out_shape=jax.ShapeDtypeStruct((2 * n_tok + tr, d), F32),
        compiler_params=_cparams(("arbitrary",)),
        name="moe_experts",
    )(te, pairs, pairs, pairs, h2, w1, w3, w2)


def _route_tables(idx, n_tok, tr):
    n_pair = 2 * n_tok
    n_tiles = n_pair // tr + N_EXPERTS
    e_flat = idx.reshape(-1)
    order = jnp.argsort(e_flat, stable=True).astype(jnp.int32)
    counts = jnp.sum((e_flat[:, None] == jnp.arange(N_EXPERTS)[None, :]).astype(jnp.int32), axis=0)
    tiles_e = (counts + tr - 1) // tr
    tile_end = jnp.cumsum(tiles_e)
    tile_start = tile_end - tiles_e
    ustart = jnp.cumsum(counts) - counts
    total = tile_end[-1]
    tile = jnp.arange(n_tiles + 1, dtype=jnp.int32)
    te = jnp.sum((jnp.minimum(tile, total - 1)[:, None] >= tile_end[None, :]).astype(jnp.int32), axis=1)
    te = jnp.clip(te, 0, N_EXPERTS - 1)
    in_e = (tile - tile_start[te]) * tr
    cnt = jnp.where(tile < total, jnp.clip(counts[te] - in_e, 0, tr), 0).astype(jnp.int32)
    base = jnp.clip(ustart[te] + in_e, 0, n_pair)
    order_pad = jnp.concatenate([order, jnp.zeros((tr,), jnp.int32)])
    pairs = jax.vmap(lambda b: lax.dynamic_slice(order_pad, (b,), (tr,)))(base)
    j = jnp.arange(tr, dtype=jnp.int32)[None, :]
    pad_ids = jnp.broadcast_to(4 * n_tok + 2 * j, (1, tr))
    pairs = jnp.where(j < cnt[:, None], pairs, pad_ids)
    pairs = jnp.concatenate([pad_ids, pairs, pad_ids], axis=0)
    return te.astype(jnp.int32), pairs.reshape(n_tiles + 3, 1, tr)


def _final_kernel(x1_ref, ya_ref, yb_ref, route_ref, mod_ref, g_ref, b_ref, o_ref, *, alpha):
    r = route_ref[...]
    f = r[:, 2:3] * ya_ref[...] + r[:, 3:4] * yb_ref[...]
    o_ref[...] = _layer_norm(alpha * x1_ref[...] + mod_ref[0, 5:6, :] * f, g_ref[...], b_ref[...])


def _moe_combine(x1, y2, route, mod, ln_g, ln_b, *, alpha, tm, tiles_per_group):
    rows = x1.shape[0]
    d = D_MODEL
    tile = lambda w: pl.BlockSpec((tm, w), lambda t: (t, 0))
    yspec = lambda s: pl.BlockSpec((tm, d), lambda t: (s * (rows // tm) + t, 0))
    return pl.pallas_call(
        functools.partial(_final_kernel, alpha=alpha),
        grid=(rows // tm,),
        in_specs=[tile(d), yspec(0), yspec(1), tile(LANES), _mod_spec(tiles_per_group, None),
                  _const_spec((1, d)), _const_spec((1, d))],
        out_specs=tile(d),
        out_shape=jax.ShapeDtypeStruct((rows, d), F32),
        compiler_params=_cparams(("parallel",)),
        name="moe_combine_norm",
    )(x1, y2, y2, route, mod, ln_g, ln_b)


def _block_diag4(m):
    z = jnp.zeros_like(m[0])
    return jnp.concatenate(
        [jnp.concatenate([m[g] if gg == g else z for gg in range(4)], axis=1) for g in range(4)], axis=0)


def _rope_tables(n):
    t = np.arange(n)
    inv_freq = jnp.power(ROPE_THETA, -jnp.arange(ROPE_FREQS, dtype=F32) / ROPE_FREQS)
    row = jnp.asarray(t // GRID_W, F32)[:, None] * inv_freq
    col = jnp.asarray(t % GRID_W, F32)[:, None] * inv_freq
    cos = jnp.concatenate([jnp.cos(row)] * 2 + [jnp.cos(col)] * 2, axis=1)
    sin = jnp.concatenate([-jnp.sin(row), jnp.sin(row), -jnp.sin(col), jnp.sin(col)], axis=1)
    return jnp.tile(cos, (1, 256 // HEAD_DIM)), jnp.tile(sin, (1, 256 // HEAD_DIM))


def _dft_tables(n, tm):
    pos = jnp.arange(n, dtype=jnp.int32)

    def cs(k):
        ang = ((k[:, None] * pos[None, :]) % n).astype(F32) * (2.0 * math.pi / n)
        return jnp.cos(ang), jnp.sin(ang)

    c2, s2 = cs(jnp.arange(tm, dtype=jnp.int32))
    c1, s1 = cs(jnp.arange(n // tm, dtype=jnp.int32) * tm)
    return c2, s2, c1[:, None, :], s1[:, None, :]


def _gqa_perm():
    perm = np.zeros(GQA_Q, np.int64)
    for j in range(GQA_GROUP):
        for g in range(GQA_KV_HEADS):
            dst = 256 * j + HEAD_DIM * g
            srcc = (GQA_GROUP * g + j) * HEAD_DIM
            perm[dst:dst + HEAD_DIM] = np.arange(srcc, srcc + HEAD_DIM)
    return perm


def kernel(x, c, ctx, c_ctx, ada_w, ada_b, ln_mix_g, ln_mix_b, ln_ffn_g, ln_ffn_b, w_out, ev_w_in, ev_pool_w, ev_pool_scale, ev_q_gain, ev_k_gain, ev_ffn_w1, ev_ffn_w3, ev_ffn_w2, od_w_in, od_fourier_gain, od_rpb, od_router, od_exp_w1, od_exp_w3, od_exp_w2):
    nb, seq, d = x.shape
    lc = ctx.shape[1]
    depth = ada_w.shape[0]
    assert d == D_MODEL and nb + 1 <= MOD_ROWS and seq % (NA_KROWS * GRID_W) == 0
    n_tok, n_ctx = nb * seq, nb * lc
    tm = 512
    tpg = seq // tm
    tm_c = min(tm, lc)
    alpha = float((2 * depth) ** 0.25)

    c_all = jnp.concatenate([c, c_ctx[None, :], jnp.zeros((MOD_ROWS - nb - 1, d), F32)], axis=0)
    mod_all = _modulation(c_all, ada_w, ada_b)

    gmat = _block_diag4(jnp.full((4, HEAD_DIM, HEAD_DIM), 1.0 / HEAD_DIM, BF16))
    tile4 = lambda g: jnp.tile(g.reshape(1, HEAD_DIM), (1, 256 // HEAD_DIM))
    row = lambda v: v.reshape(1, -1)

    xf = x.reshape(n_tok, d)
    cf = ctx.reshape(n_ctx, d)
    lat = dict(tm=tm, tiles_per_group=tpg, fixed_row=None)
    cx = dict(tm=tm_c, tiles_per_group=1, fixed_row=nb)

    for layer in range(depth):
        with_ctx = layer < depth - 1
        i = layer // 2
        mod = mod_all[layer]
        lg1, lb1 = row(ln_mix_g[layer]), row(ln_mix_b[layer])
        lg2, lb2 = row(ln_ffn_g[layer]), row(ln_ffn_b[layer])
        wo = w_out[layer].astype(BF16)
        if layer % 2 == 0:
            perm = _gqa_perm()
            w_in = ev_w_in[i]
            w_in = jnp.concatenate(
                [w_in[:, :POOL_WIDTH], w_in[:, POOL_WIDTH:POOL_WIDTH + GQA_Q][:, perm],
                 w_in[:, POOL_WIDTH + GQA_Q:]], axis=1).astype(BF16)
            wa, wb = wo[:POOL_WIDTH], wo[POOL_WIDTH:][perm]
            qg, kg = tile4(ev_q_gain[i]), tile4(ev_k_gain[i])
            cos, sin = _rope_tables(seq)
            pw = _block_diag4(ev_pool_w[i].astype(BF16))
            ps = row(ev_pool_scale[i])
            w1, w3, w2 = (ev_ffn_w1[i].astype(BF16), ev_ffn_w3[i].astype(BF16), ev_ffn_w2[i].astype(BF16))

            a, q, k, v = _inproj_even(xf, mod, w_in, gmat, qg, kg, cos, sin, rope=True, **lat)
            a_c, q_c, k_c, v_c = _inproj_even(cf, mod, w_in, gmat, qg, kg, cos, sin, rope=False, **cx)
            o = _gqa_attention(q, k, v, k_c, v_c, tq=tm, n=seq, lc=lc)
            ya = _pool(a, pw, ps, n=seq)
            x1, h2 = _outproj(xf, ya, o, mod, wa, wb, lg1, lb1, None, alpha=alpha, **lat)
            xf = _ffn(h2, x1, mod, w1, w3, w2, lg2, lb2, alpha=alpha, **lat)
            if with_ctx:
                o_c = _gqa_attention(q_c, None, None, k_c, v_c, tq=lc, n=seq, lc=lc)
                ya_c = _pool(a_c, pw, ps, n=lc)
                x1_c, h2_c = _outproj(cf, ya_c, o_c, mod, wa, wb, lg1, lb1, None, alpha=alpha, **cx)
                cf = _ffn(h2_c, x1_c, mod, w1, w3, w2, lg2, lb2, alpha=alpha, **cx)
        else:
            assert not with_ctx, "context outputs of an odd layer are not produced"
            w_in = od_w_in[i].astype(BF16)
            wa, wb = wo[:FOURIER_WIDTH], wo[FOURIER_WIDTH:]
            fg = row(od_fourier_gain[i])
            ch = np.arange(HEAD_DIM)
            ang = 2.0 * np.pi * ((ch[:, None] * ch[None, :]) % HEAD_DIM) / HEAD_DIM
            cc = _block_diag4(jnp.asarray(np.stack([np.cos(ang)] * 4), BF16))
            sc = _block_diag4(jnp.asarray(np.stack([np.sin(ang)] * 4), BF16))
            dft_tabs = _dft_tables(seq, tm)
            variants, tab = _na_tables(seq // GRID_W)
            bias = _na_bias(od_rpb[i], variants)
            w_router = jnp.pad(od_router[i], ((0, 0), (0, LANES - N_EXPERTS)))
            w1, w3, w2 = (od_exp_w1[i].astype(BF16), od_exp_w3[i].astype(BF16), od_exp_w2[i].astype(BF16))

            fa, fb, q, k, v = _inproj_odd(xf, mod, w_in, gmat, fg, cc, sc, **lat)
            _, _, _, k_c, v_c = _inproj_odd(cf, mod, w_in, gmat, fg, cc, sc, **cx)
            ya = _dft_positions(*dft_tabs, fa, fb, n=seq, tm=tm)
            o = _na_attention(q, k, v, k_c, v_c, bias, jnp.asarray(tab), n=seq, lc=lc)
            x1, h2, route = _outproj(xf, ya, o, mod, wa, wb, lg1, lb1, w_router, alpha=alpha, **lat)
            idx = route[:, :2].astype(jnp.int32)
            te, pairs = _route_tables(idx, n_tok, MOE_TILE)
            y2 = _moe_experts(h2, te, pairs, w1, w3, w2, n_tok=n_tok)
            xf = _moe_combine(x1, y2, route, mod, lg2, lb2, alpha=alpha, tm=tm, tiles_per_group=tpg)
    return xf.reshape(nb, seq, d)
```

```python
import functools
import math

import numpy as np
import jax
import jax.numpy as jnp
from jax import lax
from jax.experimental import pallas as pl
from jax.experimental.pallas import tpu as pltpu

F32 = jnp.float32
BF16 = jnp.bfloat16

D_MODEL = 1024
HEAD_DIM = 64
GRID_W = 64
POOL_WIDTH = 256
POOL_WINDOWS = (2, 4, 8, 16)
GQA_HEADS = 12
GQA_KV_HEADS = 4
GQA_GROUP = 3
GQA_Q = 768
GQA_KV = 256
FOURIER_WIDTH = 256
NA_HEADS = 12
NA_WIDTH = 768
MAX_WIN_R = 8
WIN_C = 16
ROPE_THETA = 10000.0
ROPE_FREQS = 16
FFN_DIM = 2816
N_EXPERTS = 8
LN_EPS = 1e-6
RMS_EPS = 1e-6
LOG2E = math.log2(math.e)
QK_SCALE = HEAD_DIM ** -0.5 * LOG2E

LANES = 128
MOD_ROWS = 16
VMEM_LIMIT = 56 * 1024 * 1024
NA_QROWS = 8
NA_KROWS = 16
MOE_TILE = 256


def _cparams(sem, flags=None):
    return pltpu.CompilerParams(dimension_semantics=sem, vmem_limit_bytes=VMEM_LIMIT, flags=flags)


def _const_spec(shape):
    nd = len(shape)
    return pl.BlockSpec(shape, lambda *_: (0,) * nd, pipeline_mode=pl.Buffered(1))


def _layer_norm(v, g, b):
    mu = jnp.mean(v, axis=-1, keepdims=True)
    d = v - mu
    var = jnp.mean(d * d, axis=-1, keepdims=True)
    return d * lax.rsqrt(var + LN_EPS) * g + b


def _dot(a, b):
    return jnp.dot(a, b, preferred_element_type=F32)


def _dot_nt(a, b):
    return lax.dot_general(a, b, (((1,), (1,)), ((), ())), preferred_element_type=F32)


def _lane_partial_sum(p):
    acc = p[:, :LANES]
    for j in range(1, p.shape[1] // LANES):
        acc = acc + p[:, LANES * j:LANES * (j + 1)]
    return acc


def _lane_partial_max(s):
    acc = s[:, :LANES]
    for j in range(1, s.shape[1] // LANES):
        acc = jnp.maximum(acc, s[:, LANES * j:LANES * (j + 1)])
    return acc


def _head_slot(shape):
    return lax.broadcasted_iota(jnp.int32, shape, len(shape) - 1) // HEAD_DIM


def _mod_kernel(c_ref, w_ref, b_ref, o_ref):
    c = c_ref[...]
    s = c * jax.nn.sigmoid(c)
    o_ref[...] = jnp.dot(s, w_ref[...], preferred_element_type=F32,
                         precision=lax.Precision.HIGHEST) + b_ref[...]


def _modulation(c_all, ada_w, ada_b):
    depth = ada_w.shape[0]
    d = D_MODEL
    out = pl.pallas_call(
        _mod_kernel,
        grid=(depth, 6),
        in_specs=[
            pl.BlockSpec((MOD_ROWS, d), lambda l, n: (0, 0)),
            pl.BlockSpec((None, d, d), lambda l, n: (l, 0, n)),
            pl.BlockSpec((None, 1, d), lambda l, n: (l, 0, n)),
        ],
        out_specs=pl.BlockSpec((None, MOD_ROWS, d), lambda l, n: (l, 0, n)),
        out_shape=jax.ShapeDtypeStruct((depth, MOD_ROWS, 6 * d), F32),
        compiler_params=_cparams(("arbitrary", "arbitrary")),
        name="adaln_mod",
    )(c_all, ada_w, ada_b.reshape(depth, 1, 6 * d))
    return out.reshape(depth, MOD_ROWS, 6, d)


def _mod_spec(tiles_per_group, fixed_row):
    if fixed_row is None:
        return pl.BlockSpec((1, 6, D_MODEL), lambda t: (t // tiles_per_group, 0, 0))
    return pl.BlockSpec((1, 6, D_MODEL), lambda t: (fixed_row, 0, 0))


def _group_norm(u, gmat, gain):
    ms = _dot((u * u).astype(BF16), gmat)
    return u * lax.rsqrt(ms + RMS_EPS) * gain


def _rope(u, cos, sin):
    lane = lax.broadcasted_iota(jnp.int32, u.shape, 1)
    first = (lane // ROPE_FREQS) % 2 == 0
    n = u.shape[1]
    partner = jnp.where(first, pltpu.roll(u, n - ROPE_FREQS, 1), pltpu.roll(u, ROPE_FREQS, 1))
    return u * cos + partner * sin


def _inproj_even_kernel(x_ref, mod_ref, w_ref, g_ref, qg_ref, kg_ref, cos_ref, sin_ref,
                        a_ref, q_ref, k_ref, v_ref, *, rope):
    x = x_ref[...]
    h = (x * (1.0 + mod_ref[0, 1:2, :]) + mod_ref[0, 0:1, :]).astype(BF16)
    p = _dot(h, w_ref[...])
    a_ref[...] = p[:, :POOL_WIDTH]
    gmat = g_ref[...]
    if rope:
        cos = cos_ref[...]
        sin = sin_ref[...]
    for c in range(GQA_Q // 256):
        u = _group_norm(p[:, POOL_WIDTH + 256 * c:POOL_WIDTH + 256 * (c + 1)], gmat, qg_ref[...])
        if rope:
            u = _rope(u, cos, sin)
        q_ref[:, 256 * c:256 * (c + 1)] = (u * QK_SCALE).astype(BF16)
    u = _group_norm(p[:, POOL_WIDTH + GQA_Q:POOL_WIDTH + GQA_Q + GQA_KV], gmat, kg_ref[...])
    if rope:
        u = _rope(u, cos, sin)
    k_ref[...] = u.astype(BF16)
    v_ref[...] = p[:, POOL_WIDTH + GQA_Q + GQA_KV:].astype(BF16)


def _inproj_even(xf, mod, w_in, gmat, qg, kg, cos, sin, *, tm, tiles_per_group, fixed_row, rope):
    rows = xf.shape[0]
    d = D_MODEL
    n_in = w_in.shape[1]
    tile = lambda w: pl.BlockSpec((tm, w), lambda t: (t, 0))
    if rope:
        tab = pl.BlockSpec((tm, 256), lambda t: (t % tiles_per_group, 0))
    else:
        tab = pl.BlockSpec((tm, 256), lambda t: (0, 0))
    return pl.pallas_call(
        functools.partial(_inproj_even_kernel, rope=rope),
        grid=(rows // tm,),
        in_specs=[tile(d), _mod_spec(tiles_per_group, fixed_row), _const_spec((d, n_in)),
                  _const_spec((256, 256)), _const_spec((1, 256)), _const_spec((1, 256)), tab, tab],
        out_specs=[tile(POOL_WIDTH), tile(GQA_Q), tile(GQA_KV), tile(GQA_KV)],
        out_shape=[jax.ShapeDtypeStruct((rows, POOL_WIDTH), F32),
                   jax.ShapeDtypeStruct((rows, GQA_Q), BF16),
                   jax.ShapeDtypeStruct((rows, GQA_KV), BF16),
                   jax.ShapeDtypeStruct((rows, GQA_KV), BF16)],
        compiler_params=_cparams(("parallel",)),
        name="inproj_even",
    )(xf, mod, w_in, gmat, qg, kg, cos, sin)


GQA_KCHUNK = 2048


def _gqa_kernel(*refs, n_lat):
    if n_lat:
        q_ref, k_ref, v_ref, kc_ref, vc_ref, o_ref = refs
    else:
        q_ref, kc_ref, vc_ref, o_ref = refs
    kchunk = min(GQA_KCHUNK, n_lat)
    q = q_ref[...]
    slot = _head_slot(q.shape)
    kc = kc_ref[...]
    vc = vc_ref[...]

    def head(g, acc):
        qm = jnp.where(slot == g, q, jnp.zeros_like(q))
        s = _dot_nt(qm, kc)
        m = s.max(axis=-1, keepdims=True)
        p = jnp.exp2(s - m)
        l = _lane_partial_sum(p)
        o = _dot(p.astype(BF16), vc)

        def chunk(ci, carry):
            m, l, o = carry
            start = pl.multiple_of(ci * kchunk, kchunk)
            s = _dot_nt(qm, k_ref[pl.ds(start, kchunk), :])
            m_new = jnp.maximum(m, s.max(axis=-1, keepdims=True))
            corr = jnp.exp2(m - m_new)
            p = jnp.exp2(s - m_new)
            l = l * corr + _lane_partial_sum(p)
            o = o * corr + _dot(p.astype(BF16), v_ref[pl.ds(start, kchunk), :])
            return m_new, l, o

        if n_lat:
            m, l, o = lax.fori_loop(0, n_lat // kchunk, chunk, (m, l, o), unroll=True)
        return jnp.where(slot == g, o / l.sum(axis=-1, keepdims=True), acc)

    acc = lax.fori_loop(0, GQA_KV_HEADS, head, jnp.zeros(q.shape, F32), unroll=True)
    o_ref[...] = acc.astype(BF16)


def _gqa_attention(q, k, v, kc, vc, *, tq, n, lc):
    rows = q.shape[0]
    assert n % min(GQA_KCHUNK, n) == 0
    q_per_batch = n if k is not None else lc
    nq = q_per_batch // tq
    qspec = pl.BlockSpec((tq, 256), lambda b, i, c: (b * nq + i, c))
    cspec = pl.BlockSpec((lc, GQA_KV), lambda b, i, c: (b, 0))
    in_specs, args = [qspec], [q]
    if k is not None:
        lspec = pl.BlockSpec((n, GQA_KV), lambda b, i, c: (b, 0))
        in_specs += [lspec, lspec]
        args += [k, v]
    in_specs += [cspec, cspec]
    args += [kc, vc]
    return pl.pallas_call(
        functools.partial(_gqa_kernel, n_lat=n if k is not None else 0),
        grid=(rows // q_per_batch, nq, GQA_Q // 256),
        in_specs=in_specs,
        out_specs=qspec,
        out_shape=jax.ShapeDtypeStruct((rows, GQA_Q), BF16),
        compiler_params=_cparams(("parallel", "arbitrary", "arbitrary")),
        name="gqa_attention",
    )(*args)


POOL_CHUNK = 512
POOL_HALO = 8


def _pool_kernel(a_ref, pw_ref, ps_ref, o_ref, ext_ref, *, n):
    chunk = min(POOL_CHUNK, n)
    zeros = jnp.zeros((POOL_HALO, POOL_WIDTH), F32)
    ext_ref[0:POOL_HALO, :] = zeros
    ext_ref[n + POOL_HALO:n + 2 * POOL_HALO, :] = zeros
    ext_ref[POOL_HALO:n + POOL_HALO, :] = a_ref[...]
    ln = chunk + 2 * POOL_HALO
    lane_grp = _head_slot((chunk, POOL_WIDTH))
    win = jnp.where(lane_grp == 0, POOL_WINDOWS[0],
                    jnp.where(lane_grp == 1, POOL_WINDOWS[1],
                              jnp.where(lane_grp == 2, POOL_WINDOWS[2], POOL_WINDOWS[3])))
    pw = pw_ref[...]
    ps = ps_ref[...]

    def body(ci, carry):
        start = pl.multiple_of(ci * chunk, chunk)
        e = ext_ref[pl.ds(start, ln), :]
        s2 = e + pltpu.roll(e, 1, 0)
        s4 = pltpu.roll(s2, 1, 0) + pltpu.roll(s2, ln - 1, 0)
        s8 = pltpu.roll(s4, 2, 0) + pltpu.roll(s4, ln - 2, 0)
        s16 = pltpu.roll(s8, 4, 0) + pltpu.roll(s8, ln - 4, 0)
        cut = lambda z: z[POOL_HALO:POOL_HALO + chunk, :]
        tot = jnp.where(lane_grp == 0, cut(s2),
                        jnp.where(lane_grp == 1, cut(s4),
                                  jnp.where(lane_grp == 2, cut(s8), cut(s16))))
        t = lax.broadcasted_iota(jnp.int32, (chunk, POOL_WIDTH), 0) + start
        lo = jnp.clip(t - win // 2, 0, n)
        hi = jnp.clip(t - win // 2 + win, 0, n)
        mean = tot / (hi - lo).astype(F32)
        m = mean - cut(e)
        o_ref[pl.ds(start, chunk), :] = (_dot(m.astype(BF16), pw) * ps).astype(BF16)
        return carry

    lax.fori_loop(0, n // chunk, body, 0)


def _pool(a, pw_bd, ps, *, n):
    rows = a.shape[0]
    return pl.pallas_call(
        functools.partial(_pool_kernel, n=n),
        grid=(rows // n,),
        in_specs=[pl.BlockSpec((n, POOL_WIDTH), lambda b: (b, 0)),
                  _const_spec((POOL_WIDTH, POOL_WIDTH)), _const_spec((1, POOL_WIDTH))],
        out_specs=pl.BlockSpec((n, POOL_WIDTH), lambda b: (b, 0)),
        out_shape=jax.ShapeDtypeStruct((rows, POOL_WIDTH), BF16),
        scratch_shapes=[pltpu.VMEM((n + 2 * POOL_HALO, POOL_WIDTH), F32)],
        compiler_params=_cparams(("parallel",)),
        name="multiscale_pool",
    )(a, pw_bd, ps)


def _outproj(xf, ya, yb, mod, wa, wb, ln_g, ln_b, w_router, *, alpha, tm, tiles_per_group, fixed_row):
    rows = xf.shape[0]
    d = D_MODEL
    router = w_router is not None
    tile = lambda w: pl.BlockSpec((tm, w), lambda t: (t, 0))
    in_specs = [tile(d), tile(ya.shape[1]), tile(yb.shape[1]), _mod_spec(tiles_per_group, fixed_row),
                _const_spec(wa.shape), _const_spec(wb.shape), _const_spec((1, d)), _const_spec((1, d))]
    args = [xf, ya, yb, mod, wa, wb, ln_g, ln_b]
    out_specs = [tile(d), tile(d)]
    out_shape = [jax.ShapeDtypeStruct((rows, d), F32),
                 jax.ShapeDtypeStruct((rows, d), F32 if router else BF16)]
    if router:
        wr_hi = w_router.astype(BF16)
        wr_lo = (w_router - wr_hi.astype(F32)).astype(BF16)
        in_specs += [_const_spec(w_router.shape)] * 2
        args += [wr_hi, wr_lo]
        out_specs.append(tile(LANES))
        out_shape.append(jax.ShapeDtypeStruct((rows, LANES), F32))
    return pl.pallas_call(
        functools.partial(_outproj_body, router=router, alpha=alpha),
        grid=(rows // tm,),
        in_specs=in_specs,
        out_specs=out_specs,
        out_shape=out_shape,
        compiler_params=_cparams(("parallel",)),
        name="outproj_norm",
    )(*args)


def _outproj_body(*refs, router, alpha):
    x_ref, ya_ref, yb_ref, mod_ref, wa_ref, wb_ref, g_ref, b_ref = refs[:8]
    if router:
        wrh_ref, wrl_ref, x1_ref, h2_ref, route_ref = refs[8:]
    else:
        x1_ref, h2_ref = refs[8:]
    y = _dot(ya_ref[...], wa_ref[...]) + _dot(yb_ref[...], wb_ref[...])
    x1 = _layer_norm(alpha * x_ref[...] + mod_ref[0, 2:3, :] * y, g_ref[...], b_ref[...])
    x1_ref[...] = x1
    h2 = x1 * (1.0 + mod_ref[0, 4:5, :]) + mod_ref[0, 3:4, :]
    h2_ref[...] = h2.astype(h2_ref.dtype)
    if router:
        xh = h2.astype(BF16)
        xl = (h2 - xh.astype(F32)).astype(BF16)
        logits = _dot(xh, wrh_ref[...]) + (_dot(xh, wrl_ref[...]) + _dot(xl, wrh_ref[...]))
        lane = lax.broadcasted_iota(jnp.int32, logits.shape, 1)
        neg = jnp.float32(-jnp.inf)
        lg = jnp.where(lane < N_EXPERTS, logits, neg)
        m1 = lg.max(axis=-1, keepdims=True)
        i1 = jnp.where(lg == m1, lane, LANES).min(axis=-1, keepdims=True)
        lg2 = jnp.where(lane == i1, neg, lg)
        m2 = lg2.max(axis=-1, keepdims=True)
        i2 = jnp.where(lg2 == m2, lane, LANES).min(axis=-1, keepdims=True)
        e = jnp.exp(m2 - m1)
        w1 = 1.0 / (1.0 + e)
        w2 = e / (1.0 + e)
        route_ref[...] = jnp.where(lane == 0, i1.astype(F32),
                                   jnp.where(lane == 1, i2.astype(F32),
                                             jnp.where(lane == 2, w1,
                                                       jnp.where(lane == 3, w2, 0.0))))


def _mix_ffn_kernel(x_ref, ya_ref, yb_ref, mod_ref, wa_ref, wb_ref, g1_ref, b1_ref,
                    w1_ref, w3_ref, w2_ref, g2_ref, b2_ref, o_ref, *, alpha):
    y = _dot(ya_ref[...], wa_ref[...]) + _dot(yb_ref[...], wb_ref[...])
    x1 = _layer_norm(alpha * x_ref[...] + mod_ref[0, 2:3, :] * y, g1_ref[...], b1_ref[...])
    h = (x1 * (1.0 + mod_ref[0, 4:5, :]) + mod_ref[0, 3:4, :]).astype(BF16)
    g = _dot(h, w1_ref[...])
    u = _dot(h, w3_ref[...])
    act = (g * jax.nn.sigmoid(g) * u).astype(BF16)
    f = _dot(act, w2_ref[...])
    o_ref[...] = _layer_norm(alpha * x1 + mod_ref[0, 5:6, :] * f, g2_ref[...], b2_ref[...])


def _mix_ffn(xf, ya, yb, mod, wa, wb, g1, b1, w1, w3, w2, g2, b2, *, alpha, tm, tiles_per_group, fixed_row):
    rows = xf.shape[0]
    d = D_MODEL
    tile = lambda w: pl.BlockSpec((tm, w), lambda t: (t, 0))
    vec = _const_spec((1, d))
    return pl.pallas_call(
        functools.partial(_mix_ffn_kernel, alpha=alpha),
        grid=(rows // tm,),
        in_specs=[tile(d), tile(ya.shape[1]), tile(yb.shape[1]), _mod_spec(tiles_per_group, fixed_row),
                  _const_spec(wa.shape), _const_spec(wb.shape), vec, vec,
                  _const_spec(w1.shape), _const_spec(w3.shape), _const_spec(w2.shape), vec, vec],
        out_specs=tile(d),
        out_shape=jax.ShapeDtypeStruct((rows, d), F32),
        compiler_params=_cparams(("parallel",)),
        name="outproj_swiglu_norm",
    )(xf, ya, yb, mod, wa, wb, g1, b1, w1, w3, w2, g2, b2)


def _inproj_odd_kernel(x_ref, mod_ref, w_ref, g_ref, fg_ref, cc_ref, sc_ref,
                       fa_ref, fb_ref, q_ref, k_ref, v_ref):
    x = x_ref[...]
    h = (x * (1.0 + mod_ref[0, 1:2, :]) + mod_ref[0, 0:1, :]).astype(BF16)
    p = _dot(h, w_ref[...])
    f = _group_norm(p[:, :FOURIER_WIDTH], g_ref[...], fg_ref[...]).astype(BF16)
    fa_ref[...] = _dot(f, cc_ref[...]).astype(BF16)
    fb_ref[...] = _dot(f, sc_ref[...]).astype(BF16)
    o = FOURIER_WIDTH
    q_ref[...] = (p[:, o:o + NA_WIDTH] * QK_SCALE).astype(BF16)
    k_ref[...] = p[:, o + NA_WIDTH:o + 2 * NA_WIDTH].astype(BF16)
    v_ref[...] = p[:, o + 2 * NA_WIDTH:].astype(BF16)


def _inproj_odd(xf, mod, w_in, gmat, fg, cc, sc, *, tm, tiles_per_group, fixed_row):
    rows = xf.shape[0]
    d = D_MODEL
    tile = lambda w: pl.BlockSpec((tm, w), lambda t: (t, 0))
    widths = (FOURIER_WIDTH, FOURIER_WIDTH, NA_WIDTH, NA_WIDTH, NA_WIDTH)
    return pl.pallas_call(
        _inproj_odd_kernel,
        grid=(rows // tm,),
        in_specs=[tile(d), _mod_spec(tiles_per_group, fixed_row), _const_spec(w_in.shape),
                  _const_spec((256, 256)), _const_spec((1, 256)),
                  _const_spec((256, 256)), _const_spec((256, 256))],
        out_specs=[tile(w) for w in widths],
        out_shape=[jax.ShapeDtypeStruct((rows, w), BF16) for w in widths],
        compiler_params=_cparams(("parallel",)),
        name="inproj_odd",
    )(xf, mod, w_in, gmat, fg, cc, sc)


def _dft_kernel(c2_ref, s2_ref, c1_ref, s1_ref, a_ref, b_ref, o_ref, cn_ref, sn_ref, *, scale):
    @pl.when(pl.program_id(1) == 0)
    def _():
        c1, s1 = c1_ref[0], s1_ref[0]
        c2, s2 = c2_ref[...], s2_ref[...]
        cn_ref[...] = (c2 * c1 - s2 * s1).astype(BF16)
        sn_ref[...] = (-(s2 * c1 + c2 * s1)).astype(BF16)

    o = _dot(cn_ref[...], a_ref[...]) + _dot(sn_ref[...], b_ref[...])
    o_ref[...] = (o * scale).astype(BF16)


def _dft_positions(c2, s2, c1, s1, fa, fb, *, n, tm):
    rows = fa.shape[0]
    nm = n // tm
    blk = pl.BlockSpec((1, 1, n), lambda m, b: (m, 0, 0))
    return pl.pallas_call(
        functools.partial(_dft_kernel, scale=float((n * HEAD_DIM) ** -0.5)),
        grid=(nm, rows // n),
        in_specs=[_const_spec((tm, n)), _const_spec((tm, n)), blk, blk,
                  pl.BlockSpec((n, FOURIER_WIDTH), lambda m, b: (b, 0)),
                  pl.BlockSpec((n, FOURIER_WIDTH), lambda m, b: (b, 0))],
        out_specs=pl.BlockSpec((tm, FOURIER_WIDTH), lambda m, b: (b * nm + m, 0)),
        out_shape=jax.ShapeDtypeStruct((rows, FOURIER_WIDTH), BF16),
        scratch_shapes=[pltpu.VMEM((tm, n), BF16), pltpu.VMEM((tm, n), BF16)],
        compiler_params=_cparams(("arbitrary", "arbitrary")),
        name="dft_positions",
    )(c2, s2, c1, s1, fa, fb)


NA_KPIECE = 256
NA_NPIECE = NA_KROWS * GRID_W // NA_KPIECE


def _na_fill_bias(tp_ref, bias_ref, d, ls):
    nh = bias_ref.shape[0]
    lane = lax.broadcasted_iota(jnp.int32, (nh, GRID_W, 2 * GRID_W), 2)
    neg = jnp.full((nh, GRID_W, 2 * GRID_W), -jnp.inf, F32)
    for i in range(NA_QROWS):
        for p in range(NA_KROWS // 2):
            ok0 = ls[i] <= 2 * p < ls[i] + MAX_WIN_R
            ok1 = ls[i] <= 2 * p + 1 < ls[i] + MAX_WIN_R
            if ok0 or ok1:
                tile = tp_ref[:, 2 * p - i + d + 1]
                if not ok0:
                    tile = jnp.where(lane >= GRID_W, tile, neg)
                if not ok1:
                    tile = jnp.where(lane < GRID_W, tile, neg)
            else:
                tile = neg
            bias_ref[:, GRID_W * i:GRID_W * (i + 1), 2 * GRID_W * p:2 * GRID_W * (p + 1)] = tile


def _na_kernel(tab_ref, *refs, variants):
    q_ref = refs[0]
    k_refs = refs[1:1 + NA_NPIECE]
    v_refs = refs[1 + NA_NPIECE:1 + 2 * NA_NPIECE]
    kc_ref, vc_ref, tp_ref, o_ref, bias_ref = refs[1 + 2 * NA_NPIECE:]

    @pl.when(pl.program_id(2) == 0)
    def _():
        var = tab_ref[0, pl.program_id(0)]
        for vi, (d, ls) in enumerate(variants):
            @pl.when(var == vi)
            def _():
                _na_fill_bias(tp_ref, bias_ref, d, ls)

    q = q_ref[...]
    slot = _head_slot(q.shape)
    kall = jnp.concatenate([r[...] for r in k_refs] + [kc_ref[...]], axis=0)
    vall = jnp.concatenate([r[...] for r in v_refs] + [vc_ref[...]], axis=0)
    n_loc = NA_NPIECE * NA_KPIECE

    def head(hh, acc):
        qm = jnp.where(slot == hh, q, jnp.zeros_like(q))
        s = _dot_nt(qm, kall)
        s_loc = s[:, :n_loc] + bias_ref[hh]
        s_ctx = s[:, n_loc:]
        m = jnp.maximum(_lane_partial_max(s_loc), _lane_partial_max(s_ctx)).max(axis=-1, keepdims=True)
        p_loc = jnp.exp2(s_loc - m)
        p_ctx = jnp.exp2(s_ctx - m)
        l = _lane_partial_sum(p_loc) + _lane_partial_sum(p_ctx)
        p = jnp.concatenate([p_loc.astype(BF16), p_ctx.astype(BF16)], axis=1)
        o = _dot(p, vall)
        return jnp.where(slot == hh, o / l.sum(axis=-1, keepdims=True), acc)

    acc = lax.fori_loop(0, 256 // HEAD_DIM, head, jnp.zeros(q.shape, F32), unroll=True)
    o_ref[...] = acc.astype(BF16)


def _na_tables(rows):
    nj = rows // NA_QROWS
    variants, var_of_j, kb_of_j = [], [], []
    for j in range(nj):
        ks = min(max(NA_QROWS * j - MAX_WIN_R // 2, 0), rows - NA_KROWS)
        d = ks - NA_QROWS * j + MAX_WIN_R - 1
        ls = tuple(min(max(NA_QROWS * j + i - MAX_WIN_R // 2, 0), rows - MAX_WIN_R) - ks
                   for i in range(NA_QROWS))
        key = (d, ls)
        if key not in variants:
            variants.append(key)
        var_of_j.append(variants.index(key))
        kb_of_j.append(ks * GRID_W // NA_KPIECE)
    return variants, np.asarray([var_of_j, kb_of_j], np.int32)


def _na_bias_tiles(rpb):
    qc = np.arange(GRID_W)[:, None]
    kc = np.arange(GRID_W)[None, :]
    cs = np.clip(qc - WIN_C // 2, 0, GRID_W - WIN_C)
    col_ok = (kc >= cs) & (kc < cs + WIN_C)
    col_idx = np.clip(kc - qc + WIN_C - 1, 0, 2 * WIN_C - 2)
    col_sel = (col_idx[..., None] == np.arange(2 * WIN_C - 1)) & col_ok[..., None]
    tcol = jnp.einsum("hab,qkb->haqk", rpb.astype(F32), jnp.asarray(col_sel, F32),
                      precision=lax.Precision.HIGHEST)
    tcol = jnp.where(jnp.asarray(col_ok), tcol * LOG2E, -jnp.inf)
    tcol = jnp.pad(tcol, ((0, 0), (1, 1), (0, 0), (0, 0)), constant_values=-jnp.inf)
    return jnp.concatenate([tcol[:, :-1], tcol[:, 1:]], axis=-1)


def _na_attention(q, k, v, kc, vc, tiles, variants, tab, *, n, lc):
    rows = q.shape[0]
    nb = rows // n
    nq = NA_QROWS * GRID_W
    nj = n // nq
    kblocks = n // NA_KPIECE
    nchunk = NA_WIDTH // 256
    heads = 256 // HEAD_DIM
    qspec = pl.BlockSpec((nq, 256), lambda j, c, b, tab: (b * nj + j, c))

    def piece(i):
        return pl.BlockSpec((NA_KPIECE, 256), lambda j, c, b, tab: (b * kblocks + tab[1, j] + i, c))

    cspec = pl.BlockSpec((lc, 256), lambda j, c, b, tab: (b, c))
    tspec = pl.BlockSpec((heads,) + tiles.shape[1:], lambda j, c, b, tab: (c, 0, 0, 0))
    grid_spec = pltpu.PrefetchScalarGridSpec(
        num_scalar_prefetch=1,
        grid=(nj, nchunk, nb),
        in_specs=[qspec] + [piece(i) for i in range(NA_NPIECE)] * 2 + [cspec, cspec, tspec],
        out_specs=qspec,
        scratch_shapes=[pltpu.VMEM((heads, nq, NA_KROWS * GRID_W), F32)],
    )
    return pl.pallas_call(
        functools.partial(_na_kernel, variants=variants),
        grid_spec=grid_spec,
        out_shape=jax.ShapeDtypeStruct((rows, NA_WIDTH), BF16),
        compiler_params=_cparams(("arbitrary", "arbitrary", "arbitrary")),
        name="neighbourhood_attention",
    )(tab, q, *([k] * NA_NPIECE), *([v] * NA_NPIECE), kc, vc, tiles)


def _moe_kernel(te_ref, p0_ref, pn_ref, pp_ref, h_hbm, w1_ref, w3_ref, w2_ref, y_hbm,
                xbuf, ybuf, gsem, ssem, *, n_tok):
    del te_ref
    i = pl.program_id(0)
    last = pl.num_programs(0) - 1
    tr = xbuf.shape[1]
    slot = i % 2
    other = 1 - slot

    def gather(pairs, buf, r):
        tok = jnp.minimum(pairs[0, r] >> 1, n_tok - 1)
        return pltpu.make_async_copy(h_hbm.at[pl.ds(tok, 1), :], xbuf.at[buf, pl.ds(r, 1), :], gsem.at[buf])

    def scatter(pairs, buf, r):
        pair = pairs[0, r]
        dst = (pair & 1) * n_tok + (pair >> 1)
        return pltpu.make_async_copy(ybuf.at[buf, pl.ds(r, 1), :], y_hbm.at[pl.ds(dst, 1), :], ssem)

    def each_row(fn):
        for r in range(tr):
            fn(r)

    @pl.when(i == 0)
    def _():
        ybuf[...] = jnp.zeros(ybuf.shape, F32)
        each_row(lambda r: gather(p0_ref, 0, r).start())

    each_row(lambda r: gather(pn_ref, slot, r).wait())
    each_row(lambda r: gather(pn_ref, other, r).start())
    each_row(lambda r: scatter(pp_ref, other, r).start())

    x = xbuf[slot].astype(BF16)
    g = _dot(x, w1_ref[...])
    u = _dot(x, w3_ref[...])
    act = (g * jax.nn.sigmoid(g) * u).astype(BF16)
    ybuf[slot] = _dot(act, w2_ref[...])

    each_row(lambda r: scatter(pp_ref, other, r).wait())

    @pl.when(i == last)
    def _():
        each_row(lambda r: gather(pn_ref, other, r).wait())


def _moe_experts(h2, te, pairs, w1, w3, w2, *, n_tok):
    d = D_MODEL
    n_steps, tr = pairs.shape[0] - 2, pairs.shape[2]
    wspec = lambda shp: pl.BlockSpec((None,) + shp, lambda i, te: (te[i], 0, 0))
    ptile = lambda off: pl.BlockSpec((None, 1, tr), off, memory_space=pltpu.SMEM)
    first = ptile(lambda i, te: (1, 0, 0))
    nxt = ptile(lambda i, te: (i + 2, 0, 0))
    prv = ptile(lambda i, te: (i, 0, 0))
    grid_spec = pltpu.PrefetchScalarGridSpec(
        num_scalar_prefetch=1,
        grid=(n_steps,),
        in_specs=[first, nxt, prv, pl.BlockSpec(memory_space=pl.ANY),
                  wspec((d, FFN_DIM)), wspec((d, FFN_DIM)), wspec((FFN_DIM, d))],
        out_specs=pl.BlockSpec(memory_space=pl.ANY),
        scratch_shapes=[pltpu.VMEM((2, tr, d), F32), pltpu.VMEM((2, tr, d), F32),
                        pltpu.SemaphoreType.DMA((2,)), pltpu.SemaphoreType.DMA(())],
    )
    return pl.pallas_call(
        functools.partial(_moe_kernel, n_tok=n_tok),
        grid_spec=grid_spec,
        out_shape=jax.ShapeDtypeStruct((2 * n_tok + tr, d), F32),
        compiler_params=_cparams(("arbitrary",)),
        name="moe_experts",
    )(te, pairs, pairs, pairs, h2, w1, w3, w2)


def _route_tables(idx, n_tok, tr):
    n_pair = 2 * n_tok
    n_tiles = n_pair // tr + N_EXPERTS
    e_flat = idx.reshape(-1)
    order = jnp.argsort(e_flat, stable=True).astype(jnp.int32)
    counts = jnp.sum((e_flat[:, None] == jnp.arange(N_EXPERTS)[None, :]).astype(jnp.int32), axis=0)
    tiles_e = (counts + tr - 1) // tr
    tile_end = jnp.cumsum(tiles_e)
    tile_start = tile_end - tiles_e
    ustart = jnp.cumsum(counts) - counts
    total = tile_end[-1]
    tile = jnp.arange(n_tiles + 1, dtype=jnp.int32)
    te = jnp.sum((jnp.minimum(tile, total - 1)[:, None] >= tile_end[None, :]).astype(jnp.int32), axis=1)
    te = jnp.clip(te, 0, N_EXPERTS - 1)
    in_e = (tile - tile_start[te]) * tr
    cnt = jnp.where(tile < total, jnp.clip(counts[te] - in_e, 0, tr), 0).astype(jnp.int32)
    shift = tile_start * tr - ustart
    order_ext = jnp.concatenate([order, jnp.zeros(((n_tiles + 1) * tr - n_pair,), jnp.int32)])
    pairs = jnp.zeros((n_tiles + 1, tr), jnp.int32)
    for e in range(N_EXPERTS):
        pairs = jnp.where(te[:, None] == e, jnp.roll(order_ext, shift[e]).reshape(n_tiles + 1, tr), pairs)
    j = jnp.arange(tr, dtype=jnp.int32)[None, :]
    pad_ids = jnp.broadcast_to(4 * n_tok + 2 * j, (1, tr))
    pairs = jnp.where(j < cnt[:, None], pairs, pad_ids)
    pairs = jnp.concatenate([pad_ids, pairs, pad_ids], axis=0)
    return te.astype(jnp.int32), pairs.reshape(n_tiles + 3, 1, tr)


def _final_kernel(x1_ref, ya_ref, yb_ref, route_ref, mod_ref, g_ref, b_ref, o_ref, *, alpha):
    r = route_ref[...]
    f = r[:, 2:3] * ya_ref[...] + r[:, 3:4] * yb_ref[...]
    o_ref[...] = _layer_norm(alpha * x1_ref[...] + mod_ref[0, 5:6, :] * f, g_ref[...], b_ref[...])


def _moe_combine(x1, y2, route, mod, ln_g, ln_b, *, alpha, tm, tiles_per_group):
    rows = x1.shape[0]
    d = D_MODEL
    tile = lambda w: pl.BlockSpec((tm, w), lambda t: (t, 0))
    yspec = lambda s: pl.BlockSpec((tm, d), lambda t: (s * (rows // tm) + t, 0))
    return pl.pallas_call(
        functools.partial(_final_kernel, alpha=alpha),
        grid=(rows // tm,),
        in_specs=[tile(d), yspec(0), yspec(1), tile(LANES), _mod_spec(tiles_per_group, None),
                  _const_spec((1, d)), _const_spec((1, d))],
        out_specs=tile(d),
        out_shape=jax.ShapeDtypeStruct((rows, d), F32),
        compiler_params=_cparams(("parallel",)),
        name="moe_combine_norm",
    )(x1, y2, y2, route, mod, ln_g, ln_b)


def _block_diag4(m):
    z = jnp.zeros_like(m[0])
    return jnp.concatenate(
        [jnp.concatenate([m[g] if gg == g else z for gg in range(4)], axis=1) for g in range(4)], axis=0)


def _rope_tables(n):
    t = np.arange(n)
    inv_freq = jnp.power(ROPE_THETA, -jnp.arange(ROPE_FREQS, dtype=F32) / ROPE_FREQS)
    row = jnp.asarray(t // GRID_W, F32)[:, None] * inv_freq
    col = jnp.asarray(t % GRID_W, F32)[:, None] * inv_freq
    cos = jnp.concatenate([jnp.cos(row)] * 2 + [jnp.cos(col)] * 2, axis=1)
    sin = jnp.concatenate([-jnp.sin(row), jnp.sin(row), -jnp.sin(col), jnp.sin(col)], axis=1)
    return jnp.tile(cos, (1, 256 // HEAD_DIM)), jnp.tile(sin, (1, 256 // HEAD_DIM))


def _dft_tables(n, tm):
    pos = jnp.arange(n, dtype=jnp.int32)

    def cs(k):
        ang = ((k[:, None] * pos[None, :]) % n).astype(F32) * (2.0 * math.pi / n)
        return jnp.cos(ang), jnp.sin(ang)

    c2, s2 = cs(jnp.arange(tm, dtype=jnp.int32))
    c1, s1 = cs(jnp.arange(n // tm, dtype=jnp.int32) * tm)
    return c2, s2, c1[:, None, :], s1[:, None, :]


def _gqa_perm():
    perm = np.zeros(GQA_Q, np.int64)
    for j in range(GQA_GROUP):
        for g in range(GQA_KV_HEADS):
            dst = 256 * j + HEAD_DIM * g
            srcc = (GQA_GROUP * g + j) * HEAD_DIM
            perm[dst:dst + HEAD_DIM] = np.arange(srcc, srcc + HEAD_DIM)
    return perm


def kernel(x, c, ctx, c_ctx, ada_w, ada_b, ln_mix_g, ln_mix_b, ln_ffn_g, ln_ffn_b, w_out, ev_w_in, ev_pool_w, ev_pool_scale, ev_q_gain, ev_k_gain, ev_ffn_w1, ev_ffn_w3, ev_ffn_w2, od_w_in, od_fourier_gain, od_rpb, od_router, od_exp_w1, od_exp_w3, od_exp_w2):
    nb, seq, d = x.shape
    lc = ctx.shape[1]
    depth = ada_w.shape[0]
    assert d == D_MODEL and nb + 1 <= MOD_ROWS and seq % (NA_KROWS * GRID_W) == 0
    n_tok, n_ctx = nb * seq, nb * lc
    tm = 512
    tpg = seq // tm
    tm_c = min(tm, lc)
    alpha = float((2 * depth) ** 0.25)

    c_all = jnp.concatenate([c, c_ctx[None, :], jnp.zeros((MOD_ROWS - nb - 1, d), F32)], axis=0)
    mod_all = _modulation(c_all, ada_w, ada_b)

    gmat = _block_diag4(jnp.full((4, HEAD_DIM, HEAD_DIM), 1.0 / HEAD_DIM, BF16))
    tile4 = lambda g: jnp.tile(g.reshape(1, HEAD_DIM), (1, 256 // HEAD_DIM))
    row = lambda v: v.reshape(1, -1)

    xf = x.reshape(n_tok, d)
    cf = ctx.reshape(n_ctx, d)
    lat = dict(tm=tm, tiles_per_group=tpg, fixed_row=None)
    cx = dict(tm=tm_c, tiles_per_group=1, fixed_row=nb)

    for layer in range(depth):
        with_ctx = layer < depth - 1
        i = layer // 2
        mod = mod_all[layer]
        lg1, lb1 = row(ln_mix_g[layer]), row(ln_mix_b[layer])
        lg2, lb2 = row(ln_ffn_g[layer]), row(ln_ffn_b[layer])
        wo = w_out[layer].astype(BF16)
        if layer % 2 == 0:
            perm = _gqa_perm()
            w_in = ev_w_in[i]
            w_in = jnp.concatenate(
                [w_in[:, :POOL_WIDTH], w_in[:, POOL_WIDTH:POOL_WIDTH + GQA_Q][:, perm],
                 w_in[:, POOL_WIDTH + GQA_Q:]], axis=1).astype(BF16)
            wa, wb = wo[:POOL_WIDTH], wo[POOL_WIDTH:][perm]
            qg, kg = tile4(ev_q_gain[i]), tile4(ev_k_gain[i])
            cos, sin = _rope_tables(seq)
            pw = _block_diag4(ev_pool_w[i].astype(BF16))
            ps = row(ev_pool_scale[i])
            w1, w3, w2 = (ev_ffn_w1[i].astype(BF16), ev_ffn_w3[i].astype(BF16), ev_ffn_w2[i].astype(BF16))

            a, q, k, v = _inproj_even(xf, mod, w_in, gmat, qg, kg, cos, sin, rope=True, **lat)
            a_c, q_c, k_c, v_c = _inproj_even(cf, mod, w_in, gmat, qg, kg, cos, sin, rope=False, **cx)
            o = _gqa_attention(q, k, v, k_c, v_c, tq=tm, n=seq, lc=lc)
            ya = _pool(a, pw, ps, n=seq)
            xf = _mix_ffn(xf, ya, o, mod, wa, wb, lg1, lb1, w1, w3, w2, lg2, lb2, alpha=alpha, **lat)
            if with_ctx:
                o_c = _gqa_attention(q_c, None, None, k_c, v_c, tq=lc, n=seq, lc=lc)
                ya_c = _pool(a_c, pw, ps, n=lc)
                cf = _mix_ffn(cf, ya_c, o_c, mod, wa, wb, lg1, lb1, w1, w3, w2, lg2, lb2, alpha=alpha, **cx)
        else:
            assert not with_ctx, "context outputs of an odd layer are not produced"
            w_in = od_w_in[i].astype(BF16)
            wa, wb = wo[:FOURIER_WIDTH], wo[FOURIER_WIDTH:]
            fg = row(od_fourier_gain[i])
            ch = np.arange(HEAD_DIM)
            ang = 2.0 * np.pi * ((ch[:, None] * ch[None, :]) % HEAD_DIM) / HEAD_DIM
            cc = _block_diag4(jnp.asarray(np.stack([np.cos(ang)] * 4), BF16))
            sc = _block_diag4(jnp.asarray(np.stack([np.sin(ang)] * 4), BF16))
            dft_tabs = _dft_tables(seq, tm)
            variants, tab = _na_tables(seq // GRID_W)
            bias_tiles = _na_bias_tiles(od_rpb[i])
            w_router = jnp.pad(od_router[i], ((0, 0), (0, LANES - N_EXPERTS)))
            w1, w3, w2 = (od_exp_w1[i].astype(BF16), od_exp_w3[i].astype(BF16), od_exp_w2[i].astype(BF16))

            fa, fb, q, k, v = _inproj_odd(xf, mod, w_in, gmat, fg, cc, sc, **lat)
            _, _, _, k_c, v_c = _inproj_odd(cf, mod, w_in, gmat, fg, cc, sc, **cx)
            ya = _dft_positions(*dft_tabs, fa, fb, n=seq, tm=tm)
            o = _na_attention(q, k, v, k_c, v_c, bias_tiles, variants, jnp.asarray(tab), n=seq, lc=lc)
            x1, h2, route = _outproj(xf, ya, o, mod, wa, wb, lg1, lb1, w_router, alpha=alpha, **lat)
            idx = route[:, :2].astype(jnp.int32)
            te, pairs = _route_tables(idx, n_tok, MOE_TILE)
            y2 = _moe_experts(h2, te, pairs, w1, w3, w2, n_tok=n_tok)
            xf = _moe_combine(x1, y2, route, mod, lg2, lb2, alpha=alpha, tm=tm, tiles_per_group=tpg)
    return xf.reshape(nb, seq, d)
```
